```python
import jax, jax.numpy as jnp
from jax import lax
import numpy as np

D_MODEL = 1024
BATCH = 8
SEQ = 8192
DEPTH = 2

CTX_LEN = 256
GRID_W = 64

CONV_CH = 256
CONV_WIDTH = 31
MLA_HEADS = 4
MLA_NOPE = 128
MLA_ROPE = 64
MLA_V = 128
MLA_Q_RANK = 384
MLA_KV_RANK = 256
RET_HEADS = 4
RET_DK = 32
RET_DV = 64
RET_CHUNK = 128
ATTN_BLOCK = 128
MIX_WIDTH = CONV_CH + MLA_HEADS * MLA_V + RET_HEADS * RET_DV

N_EXPERTS = 32
TOP_K = 4
D_FF_EXPERT = 1024
SWIGLU_LIMIT = 7.0
SWIGLU_ALPHA = 1.702
MOE_BLOCK = 128

ROPE_BASE = 10000.0
RMS_EPS = 1e-6
LN_EPS = 1e-5

COL_CONV = 2 * CONV_CH
COL_QSIDE = MLA_Q_RANK + RET_HEADS * RET_DK + RET_HEADS * RET_DV
COL_KVSIDE = MLA_KV_RANK + MLA_ROPE + RET_HEADS * RET_DK + RET_HEADS * RET_DV
IN_COLS = COL_CONV + COL_QSIDE + COL_KVSIDE
KV_OFF = COL_CONV + COL_QSIDE

kernel_name = 'hybrid_conv_mla_retention_moe_dit'


def rms_norm(x, g):
    xf = x.astype(jnp.float32)
    y = xf * lax.rsqrt(jnp.mean(xf * xf, axis=-1, keepdims=True) + RMS_EPS)
    return (y * g.astype(jnp.float32)).astype(x.dtype)


def rms_norm_plain(x):
    xf = x.astype(jnp.float32)
    return (xf * lax.rsqrt(jnp.mean(xf * xf, axis=-1, keepdims=True) + RMS_EPS)).astype(x.dtype)


def layer_norm(x, g, b):
    xf = x.astype(jnp.float32)
    mu = jnp.mean(xf, axis=-1, keepdims=True)
    var = jnp.mean(jnp.square(xf - mu), axis=-1, keepdims=True)
    return ((xf - mu) * lax.rsqrt(var + LN_EPS) * g.astype(jnp.float32) + b.astype(jnp.float32)).astype(x.dtype)


def rotate_half(x, cos, sin):
    half = x.shape[-1] // 2
    cos = cos.astype(x.dtype)
    sin = sin.astype(x.dtype)
    x1, x2 = x[..., :half], x[..., half:]
    return jnp.concatenate([x1 * cos - x2 * sin, x2 * cos + x1 * sin], axis=-1)


def axial_rope(rows):
    row = jnp.repeat(jnp.arange(rows), GRID_W).astype(jnp.float32)
    col = jnp.tile(jnp.arange(GRID_W), rows).astype(jnp.float32)
    n_pairs_axis = MLA_ROPE // 4
    freq = ROPE_BASE ** (-jnp.arange(n_pairs_axis, dtype=jnp.float32) / n_pairs_axis)
    ang = jnp.concatenate([row[:, None] * freq, col[:, None] * freq], axis=-1)
    return jnp.cos(ang), jnp.sin(ang)


def retention_rot(pos):
    theta = 1.0 / (ROPE_BASE ** jnp.linspace(0.0, 1.0, RET_DK // 2, dtype=jnp.float32))
    ang = pos.astype(jnp.float32)[:, None] * theta
    return jnp.cos(ang), jnp.sin(ang)


def split_heads(t, n_heads):
    b, n, _ = t.shape
    return t.reshape(b, n, n_heads, -1).transpose(0, 2, 1, 3)


def merge_heads(t):
    b, h, n, d = t.shape
    return t.transpose(0, 2, 1, 3).reshape(b, n, h * d)


def conv_module(t, w_dw, b_dw, ln_g, ln_b):
    a, gate = t[..., :CONV_CH], t[..., CONV_CH:]
    z = a * jax.nn.sigmoid(gate)
    pad = CONV_WIDTH // 2
    z = lax.conv_general_dilated(z, w_dw.astype(z.dtype)[:, None, :], window_strides=(1,),
                                 padding=[(pad, pad)], dimension_numbers=('NWC', 'WIO', 'NWC'),
                                 feature_group_count=CONV_CH) + b_dw
    return jax.nn.silu(layer_norm(z, ln_g, ln_b))


def mla_query(cq, q_norm, w_uq, rope):
    q = split_heads(rms_norm(cq, q_norm) @ w_uq, MLA_HEADS)
    q_nope, q_rope = q[..., :MLA_NOPE], q[..., MLA_NOPE:]
    if rope is not None:
        q_rope = rotate_half(q_rope, *rope)
    return jnp.concatenate([q_nope, q_rope], axis=-1)


def mla_keys_values(ckv, k_rope, kv_norm, w_ukv, rope):
    kv = split_heads(rms_norm(ckv, kv_norm) @ w_ukv, MLA_HEADS)
    k_nope, v = kv[..., :MLA_NOPE], kv[..., MLA_NOPE:]
    k_rope = k_rope[:, None]
    if rope is not None:
        k_rope = rotate_half(k_rope, *rope)
    k = jnp.concatenate([k_nope, jnp.broadcast_to(k_rope, k_nope.shape[:-1] + (MLA_ROPE,))], axis=-1)
    return k, v


def block_attention(q, k, v):
    b, h, n, dq = q.shape
    nb = n // ATTN_BLOCK
    scale = dq ** -0.5
    qb = jnp.moveaxis(q.reshape(b, h, nb, ATTN_BLOCK, dq), 2, 0)

    def one_block(qblk):
        s = jnp.einsum('bhqd,bhkd->bhqk', qblk, k, preferred_element_type=jnp.float32) * scale
        p = jax.nn.softmax(s, axis=-1).astype(v.dtype)
        return jnp.einsum('bhqk,bhkd->bhqd', p, v)

    out = lax.map(one_block, qb)
    return jnp.moveaxis(out, 0, 2).reshape(b, h, n, v.shape[-1])


def retention_chunks(q, k, v, log_g, s0):
    b, h, n, dk = q.shape
    dv = v.shape[-1]
    nc = n // RET_CHUNK
    qc = q.reshape(b, h, nc, RET_CHUNK, dk)
    kc = k.reshape(b, h, nc, RET_CHUNK, dk)
    vc = v.reshape(b, h, nc, RET_CHUNK, dv)
    idx = jnp.arange(RET_CHUNK, dtype=jnp.float32)
    diff = idx[:, None] - idx[None, :]
    dmat = jnp.where(diff >= 0, jnp.exp(jnp.maximum(diff, 0.0) * log_g[:, None, None]), 0.0).astype(q.dtype)
    scores = jnp.einsum('bhnid,bhnjd->bhnij', qc, kc) * dmat[:, None]
    inner = jnp.einsum('bhnij,bhnje->bhnie', scores, vc)
    k_dec = jnp.exp((RET_CHUNK - 1 - idx) * log_g[:, None]).astype(q.dtype)
    q_dec = jnp.exp((idx + 1.0) * log_g[:, None]).astype(q.dtype)
    chunk_dec = jnp.exp(RET_CHUNK * log_g).astype(q.dtype)[:, None, None]
    chunk_kv = jnp.einsum('bhnjd,hj,bhnje->bhnde', kc, k_dec, vc)

    def step(s, kv_n):
        return s * chunk_dec + kv_n, s

    s_final, s_before = lax.scan(step, s0.astype(q.dtype), jnp.moveaxis(chunk_kv, 2, 0))
    s_before = jnp.moveaxis(s_before, 0, 2)
    cross = jnp.einsum('bhnid,hi,bhnde->bhnie', qc, q_dec, s_before)
    return (inner + cross).reshape(b, h, n, dv), s_final


def retention_state(k, v, log_g):
    n = k.shape[2]
    w = jnp.exp((n - 1 - jnp.arange(n, dtype=jnp.float32)) * log_g[:, None]).astype(k.dtype)
    return jnp.einsum('bhnd,hn,bhne->bhde', k, w, v)


def retention_output(o, g):
    return jax.nn.silu(g) * merge_heads(rms_norm_plain(o))


def clamped_swiglu(hid):
    glu, lin = hid[..., :D_FF_EXPERT], hid[..., D_FF_EXPERT:]
    glu = jnp.minimum(glu, SWIGLU_LIMIT)
    lin = jnp.clip(lin, -SWIGLU_LIMIT, SWIGLU_LIMIT)
    return glu * jax.nn.sigmoid(SWIGLU_ALPHA * glu) * (lin + 1.0)


def moe_ffn(h, router_w, router_b, w1, b1, w2, b2):
    n_tok, d = h.shape
    logits = jnp.dot(h, router_w, preferred_element_type=jnp.float32) + router_b.astype(jnp.float32)
    top_val, top_idx = lax.top_k(logits, TOP_K)
    gates = jax.nn.softmax(top_val, axis=-1).astype(h.dtype)
    n_assign = n_tok * TOP_K
    e_flat = top_idx.reshape(n_assign)
    order = jnp.argsort(e_flat)
    e_sorted = e_flat[order]
    tok_sorted = (order // TOP_K).astype(jnp.int32)
    gate_sorted = gates.reshape(n_assign)[order]
    counts = jnp.bincount(e_flat, length=N_EXPERTS)
    padded = (counts + MOE_BLOCK - 1) // MOE_BLOCK * MOE_BLOCK
    padded_end = jnp.cumsum(padded)
    group_start = jnp.cumsum(counts) - counts
    dest = (padded_end - padded)[e_sorted] + jnp.arange(n_assign) - group_start[e_sorted]
    n_blocks = -(-n_assign // MOE_BLOCK) + N_EXPERTS
    n_slots = n_blocks * MOE_BLOCK
    slot_tok = jnp.full((n_slots,), n_tok, jnp.int32).at[dest].set(tok_sorted)
    slot_gate = jnp.zeros((n_slots,), h.dtype).at[dest].set(gate_sorted)
    block_expert = jnp.minimum(
        jnp.searchsorted(padded_end, jnp.arange(n_blocks) * MOE_BLOCK, side='right'), N_EXPERTS - 1)

    def expert_block(args):
        tok, gate, e = args
        xb = jnp.take(h, tok, axis=0, mode='fill', fill_value=0)
        hid = xb @ w1[e] + b1[e]
        return (clamped_swiglu(hid) @ w2[e] + b2[e]) * gate[:, None]

    out = lax.map(expert_block, (slot_tok.reshape(n_blocks, MOE_BLOCK),
                                 slot_gate.reshape(n_blocks, MOE_BLOCK), block_expert))
    return jnp.zeros_like(h).at[slot_tok].add(out.reshape(n_slots, d), mode='drop')


def hybrid_layer(x, xc, mod, mod_c, p, rope_lat, rot_ctx, rot_lat, update_ctx):
    b, L, d = x.shape
    C = xc.shape[1]
    sh1, sc1, gt1, sh2, sc2, gt2 = jnp.split(mod[:, None, :], 6, axis=-1)
    csh1, csc1, cgt1, csh2, csc2, cgt2 = jnp.split(mod_c, 6)

    h = rms_norm(x, p['g_norm1']) * (1 + sc1) + sh1
    hc = rms_norm(xc, p['g_norm1']) * (1 + csc1) + csh1
    u = h @ p['w_in']
    if update_ctx:
        uc = hc @ p['w_in']
        uc_kv = uc[..., KV_OFF:]
    else:
        uc_kv = hc @ p['w_in'][:, KV_OFF:]

    def q_side(t):
        t = t[..., COL_CONV:KV_OFF]
        o1 = MLA_Q_RANK
        o2 = o1 + RET_HEADS * RET_DK
        return t[..., :o1], t[..., o1:o2], t[..., o2:]

    def kv_side(t):
        o1 = MLA_KV_RANK
        o2 = o1 + MLA_ROPE
        o3 = o2 + RET_HEADS * RET_DK
        return t[..., :o1], t[..., o1:o2], t[..., o2:o3], t[..., o3:]

    cq, rq, rg = q_side(u)
    ckv, krope, rk, rv = kv_side(u[..., KV_OFF:])
    ckv_c, krope_c, rk_c, rv_c = kv_side(uc_kv)

    y_conv = conv_module(u[..., :COL_CONV], p['conv_w'], p['conv_b'], p['conv_ln_g'], p['conv_ln_b'])

    q_l = mla_query(cq, p['mla_q_norm'], p['mla_w_uq'], rope_lat)
    k_l, v_l = mla_keys_values(ckv, krope, p['mla_kv_norm'], p['mla_w_ukv'], rope_lat)
    k_c, v_c = mla_keys_values(ckv_c, krope_c, p['mla_kv_norm'], p['mla_w_ukv'], None)
    o_mla = block_attention(q_l, jnp.concatenate([k_l, k_c], axis=2), jnp.concatenate([v_l, v_c], axis=2))

    log_gf = jax.nn.log_sigmoid(p['ret_decay_fwd'].astype(jnp.float32))
    log_gb = jax.nn.log_sigmoid(p['ret_decay_bwd'].astype(jnp.float32))
    k_scale = RET_DK ** -0.5
    rq_l = rotate_half(split_heads(rq, RET_HEADS), *rot_lat)
    rk_l = rotate_half(split_heads(rk, RET_HEADS), *rot_lat) * k_scale
    rv_l = split_heads(rv, RET_HEADS)
    rk_cc = rotate_half(split_heads(rk_c, RET_HEADS), *rot_ctx) * k_scale
    rv_cc = split_heads(rv_c, RET_HEADS)
    flip = lambda t: jnp.flip(t, axis=2)
    if update_ctx:
        cq_c, rq_c, rg_c = q_side(uc)
        rq_cc = rotate_half(split_heads(rq_c, RET_HEADS), *rot_ctx)
        zero_s = jnp.zeros((b, RET_HEADS, RET_DK, RET_DV), x.dtype)
        oc_f, s_f = retention_chunks(rq_cc, rk_cc, rv_cc, log_gf, zero_s)
        oc_b, s_b = retention_chunks(flip(rq_cc), flip(rk_cc), flip(rv_cc), log_gb, zero_s)
    else:
        s_f = retention_state(rk_cc, rv_cc, log_gf)
        s_b = retention_state(flip(rk_cc), flip(rv_cc), log_gb)
    o_f, _ = retention_chunks(rq_l, rk_l, rv_l, log_gf, s_f)
    o_b, _ = retention_chunks(flip(rq_l), flip(rk_l), flip(rv_l), log_gb, s_b)
    y_ret = retention_output(o_f + flip(o_b), rg)

    mix = jnp.concatenate([y_conv, merge_heads(o_mla), y_ret], axis=-1) @ p['w_out']
    x = x + gt1 * mix

    if update_ctx:
        y_conv_c = conv_module(uc[..., :COL_CONV], p['conv_w'], p['conv_b'], p['conv_ln_g'], p['conv_ln_b'])
        q_c = mla_query(cq_c, p['mla_q_norm'], p['mla_w_uq'], None)
        o_mla_c = block_attention(q_c, k_c, v_c)
        y_ret_c = retention_output(oc_f + flip(oc_b), rg_c)
        mix_c = jnp.concatenate([y_conv_c, merge_heads(o_mla_c), y_ret_c], axis=-1) @ p['w_out']
        xc = xc + cgt1 * mix_c

    h2 = rms_norm(x, p['g_norm2']) * (1 + sc2) + sh2
    moe_args = (p['router_w'], p['router_b'], p['moe_w1'], p['moe_b1'], p['moe_w2'], p['moe_b2'])
    if update_ctx:
        h2c = rms_norm(xc, p['g_norm2']) * (1 + csc2) + csh2
        tokens = jnp.concatenate([h2.reshape(b * L, d), h2c.reshape(b * C, d)], axis=0)
        y = moe_ffn(tokens, *moe_args)
        x = x + gt2 * y[:b * L].reshape(b, L, d)
        xc = xc + cgt2 * y[b * L:].reshape(b, C, d)
    else:
        x = x + gt2 * moe_ffn(h2.reshape(b * L, d), *moe_args).reshape(b, L, d)
    return x, xc


def setup_inputs(seed: int = 0) -> dict:
    key = jax.random.key(seed)
    ks = jax.random.split(key, 32)
    f32 = jnp.float32
    nrm = lambda k, shape, s: jax.random.normal(k, shape, f32) * s
    D = D_MODEL
    base_logit = jnp.log(2.0 ** (5.0 + jnp.arange(RET_HEADS, dtype=f32)) - 1.0)
    return {
        'x': nrm(ks[0], (BATCH, SEQ, D), 1.0),
        'c': nrm(ks[1], (BATCH, D), 1.0),
        'ctx': nrm(ks[2], (BATCH, CTX_LEN, D), 1.0),
        'c_ctx': nrm(ks[3], (D,), 1.0),
        'w_ada': nrm(ks[4], (DEPTH, D, 6 * D), 0.5 * D ** -0.5),
        'b_ada': nrm(ks[5], (DEPTH, 6 * D), 0.02),
        'g_norm1': 1.0 + nrm(ks[6], (DEPTH, D), 0.05),
        'g_norm2': 1.0 + nrm(ks[7], (DEPTH, D), 0.05),
        'w_in': nrm(ks[8], (DEPTH, D, IN_COLS), D ** -0.5),
        'w_out': nrm(ks[9], (DEPTH, MIX_WIDTH, D), MIX_WIDTH ** -0.5),
        'conv_w': nrm(ks[10], (DEPTH, CONV_WIDTH, CONV_CH), CONV_WIDTH ** -0.5),
        'conv_b': nrm(ks[11], (DEPTH, CONV_CH), 0.02),
        'conv_ln_g': 1.0 + nrm(ks[12], (DEPTH, CONV_CH), 0.05),
        'conv_ln_b': nrm(ks[13], (DEPTH, CONV_CH), 0.02),
        'mla_q_norm': 1.0 + nrm(ks[14], (DEPTH, MLA_Q_RANK), 0.05),
        'mla_w_uq': nrm(ks[15], (DEPTH, MLA_Q_RANK, MLA_HEADS * (MLA_NOPE + MLA_ROPE)), MLA_Q_RANK ** -0.5),
        'mla_kv_norm': 1.0 + nrm(ks[16], (DEPTH, MLA_KV_RANK), 0.05),
        'mla_w_ukv': nrm(ks[17], (DEPTH, MLA_KV_RANK, MLA_HEADS * (MLA_NOPE + MLA_V)), MLA_KV_RANK ** -0.5),
        'ret_decay_fwd': base_logit + nrm(ks[18], (DEPTH, RET_HEADS), 0.1),
        'ret_decay_bwd': base_logit + nrm(ks[19], (DEPTH, RET_HEADS), 0.1),
        'router_w': nrm(ks[20], (DEPTH, D, N_EXPERTS), D ** -0.5),
        'router_b': nrm(ks[21], (DEPTH, N_EXPERTS), 0.01),
        'moe_w1': nrm(ks[22], (DEPTH, N_EXPERTS, D, 2 * D_FF_EXPERT), D ** -0.5),
        'moe_b1': nrm(ks[23], (DEPTH, N_EXPERTS, 2 * D_FF_EXPERT), 0.02),
        'moe_w2': nrm(ks[24], (DEPTH, N_EXPERTS, D_FF_EXPERT, D), D_FF_EXPERT ** -0.5),
        'moe_b2': nrm(ks[25], (DEPTH, N_EXPERTS, D), 0.02),
        'g_final': 1.0 + nrm(ks[26], (D,), 0.05),
    }


def reference(x, c, ctx, c_ctx, w_ada, b_ada, g_norm1, g_norm2, w_in, w_out, conv_w, conv_b,
              conv_ln_g, conv_ln_b, mla_q_norm, mla_w_uq, mla_kv_norm, mla_w_ukv, ret_decay_fwd,
              ret_decay_bwd, router_w, router_b, moe_w1, moe_b1, moe_w2, moe_b2, g_final):
    L = x.shape[1]
    C = ctx.shape[1]
    ROWS = L // GRID_W
    rope_lat = axial_rope(ROWS)
    rot_ctx = retention_rot(jnp.arange(C))
    rot_lat = retention_rot(C + jnp.arange(L))
    s_c = jax.nn.silu(c)
    s_cc = jax.nn.silu(c_ctx)
    xc = ctx
    for l in range(DEPTH):
        mod = s_c @ w_ada[l] + b_ada[l]
        mod_c = s_cc @ w_ada[l] + b_ada[l]
        p = {
            'g_norm1': g_norm1[l], 'g_norm2': g_norm2[l], 'w_in': w_in[l], 'w_out': w_out[l],
            'conv_w': conv_w[l], 'conv_b': conv_b[l], 'conv_ln_g': conv_ln_g[l], 'conv_ln_b': conv_ln_b[l],
            'mla_q_norm': mla_q_norm[l], 'mla_w_uq': mla_w_uq[l], 'mla_kv_norm': mla_kv_norm[l],
            'mla_w_ukv': mla_w_ukv[l], 'ret_decay_fwd': ret_decay_fwd[l], 'ret_decay_bwd': ret_decay_bwd[l],
            'router_w': router_w[l], 'router_b': router_b[l], 'moe_w1': moe_w1[l], 'moe_b1': moe_b1[l],
            'moe_w2': moe_w2[l], 'moe_b2': moe_b2[l],
        }
        x, xc = hybrid_layer(x, xc, mod, mod_c, p, rope_lat, rot_ctx, rot_lat, l < DEPTH - 1)
    return rms_norm(x, g_final)
```

```python
import functools

import jax
import jax.numpy as jnp
from jax import lax
from jax.experimental import pallas as pl
from jax.experimental.pallas import tpu as pltpu

F32 = jnp.float32
BF16 = jnp.bfloat16

D_MODEL = 1024
CONV_CH = 256
CONV_WIDTH = 31
MLA_HEADS = 4
MLA_NOPE = 128
MLA_ROPE = 64
MLA_V = 128
MLA_Q_RANK = 384
MLA_KV_RANK = 256
RET_HEADS = 4
RET_DK = 32
RET_DV = 64
N_EXPERTS = 32
TOP_K = 4
D_FF = 1024
SWIGLU_LIMIT = 7.0
SWIGLU_ALPHA = 1.702
GRID_W = 64
ROPE_BASE = 10000.0
RMS_EPS = 1e-6
LN_EPS = 1e-5

LANE = 128
QK_DIM = 2 * LANE
RET_QK = RET_HEADS * RET_DK
RET_V = RET_HEADS * RET_DV
CONV_HALO = 16
CONV_SUB = 256
RET_CHUNK = 128
NEG_BIG = -1e30

_O_A, _O_G, _O_CQ, _O_RQ, _O_RQT, _O_RG, _O_CKV, _O_RK, _O_RKT, _O_RV, _O_KR, _O_KRT, IN_EXT = (
    0, 256, 512, 896, 1024, 1152, 1408, 1664, 1792, 1920, 2176, 2304, 2432)

VMEM_LIMIT = 56 * 1024 * 1024


def _cparams(sem):
    return pltpu.CompilerParams(dimension_semantics=sem, vmem_limit_bytes=VMEM_LIMIT)


def _rms(x, eps=RMS_EPS):
    return x * lax.rsqrt(jnp.mean(x * x, axis=-1, keepdims=True) + eps)


def _dot(a, b, **kw):
    return jnp.dot(a, b, preferred_element_type=F32, **kw)


def _dot_nt(a, b):
    return lax.dot_general(a, b, (((1,), (1,)), ((), ())), preferred_element_type=F32)


def _mod_kernel(cs_ref, w_ref, b_ref, o_ref):
    s = cs_ref[...]
    s = s * jax.nn.sigmoid(s)
    o_ref[...] = _dot(s, w_ref[...], precision=lax.Precision.HIGHEST) + b_ref[...]


def _modulation(cs, w_ada, b_ada):
    depth, d, n = w_ada.shape
    r = cs.shape[0]
    tn = 1536
    return pl.pallas_call(
        _mod_kernel,
        grid=(depth, n // tn),
        in_specs=[pl.BlockSpec((r, d), lambda l, j: (0, 0)),
                  pl.BlockSpec((None, d, tn), lambda l, j: (l, 0, j)),
                  pl.BlockSpec((None, 1, tn), lambda l, j: (l, 0, j))],
        out_specs=pl.BlockSpec((None, r, tn), lambda l, j: (l, 0, j)),
        out_shape=jax.ShapeDtypeStruct((depth, r, n), F32),
        compiler_params=_cparams(("arbitrary", "arbitrary")),
        name="modulation",
    )(cs, w_ada, b_ada.reshape(depth, 1, n))


def _inproj_kernel(*refs, combine):
    if combine:
        (x_ref, y4_ref, modp_ref, mod_ref, g1_ref, win_ref, qn_ref, wuq_ref, kvn_ref, wukv_ref,
         mcs_ref, rcs_ref, xo_ref, z_ref, q_ref, k_ref, v_ref, rq_ref, rk_ref, rv_ref, rg_ref) = refs
    else:
        (x_ref, mod_ref, g1_ref, win_ref, qn_ref, wuq_ref, kvn_ref, wukv_ref,
         mcs_ref, rcs_ref, z_ref, q_ref, k_ref, v_ref, rq_ref, rk_ref, rv_ref, rg_ref) = refs
    d = D_MODEL
    x = x_ref[...]
    if combine:
        y = (y4_ref[:, 0:d] + y4_ref[:, d:2 * d]) + (y4_ref[:, 2 * d:3 * d] + y4_ref[:, 3 * d:4 * d])
        x = x + modp_ref[:, 5 * d:6 * d] * y
        xo_ref[...] = x
    h = _rms(x) * g1_ref[...] * (1.0 + mod_ref[:, d:2 * d]) + mod_ref[:, 0:d]
    u = _dot(h.astype(BF16), win_ref[...])

    z_ref[...] = u[:, _O_A:_O_G] * jax.nn.sigmoid(u[:, _O_G:_O_CQ])

    mc = mcs_ref[:, 0:LANE]
    ms = mcs_ref[:, LANE:2 * LANE]
    rc = rcs_ref[:, 0:LANE]
    rs = rcs_ref[:, LANE:2 * LANE]

    cqn = _rms(u[:, _O_CQ:_O_RQ]) * qn_ref[...]
    uq = _dot(cqn.astype(BF16), wuq_ref[...])
    scale = float(MLA_NOPE + MLA_ROPE) ** -0.5
    hw = MLA_HEADS * LANE
    for hd in range(MLA_HEADS):
        lo = hd * LANE
        q_ref[hd, :, 0:LANE] = (uq[:, lo:lo + LANE] * scale).astype(BF16)
        rope = uq[:, hw + lo:hw + lo + LANE] * mc + uq[:, 2 * hw + lo:2 * hw + lo + LANE] * ms
        q_ref[hd, :, LANE:2 * LANE] = (rope * scale).astype(BF16)

    ckvn = _rms(u[:, _O_CKV:_O_RK]) * kvn_ref[...]
    ukv = _dot(ckvn.astype(BF16), wukv_ref[...])
    krope = (u[:, _O_KR:_O_KRT] * mc + u[:, _O_KRT:IN_EXT] * ms).astype(BF16)
    for hd in range(MLA_HEADS):
        lo = hd * LANE
        k_ref[hd, :, 0:LANE] = ukv[:, lo:lo + LANE].astype(BF16)
        k_ref[hd, :, LANE:2 * LANE] = krope
        v_ref[hd] = ukv[:, hw + lo:hw + lo + LANE].astype(BF16)

    rq_ref[...] = (u[:, _O_RQ:_O_RQT] * rc + u[:, _O_RQT:_O_RG] * rs).astype(BF16)
    rk_ref[...] = ((u[:, _O_RK:_O_RKT] * rc + u[:, _O_RKT:_O_RV] * rs) * (float(RET_DK) ** -0.5)).astype(BF16)
    rv_ref[...] = u[:, _O_RV:_O_KR].astype(BF16)
    rg_ref[...] = u[:, _O_RG:_O_CKV]


def _inproj(x, mod3, lw, tabs, geo, y4=None, modp3=None):
    nt, tm, nlt, lt = geo["NT"], geo["TM"], geo["NLT"], geo["LT"]
    b, l = geo["B"], geo["L"]
    d = D_MODEL
    combine = y4 is not None
    row = lambda t: (t, 0)
    const2 = lambda t: (0, 0)
    modrow = lambda t: (jnp.where(t < nlt, (t * tm) // l, b), 0, 0)
    tabrow = lambda t: (jnp.where(t < nlt, t % lt, lt), 0)
    in_specs = [pl.BlockSpec((tm, d), row)]
    args = [x]
    if combine:
        in_specs += [pl.BlockSpec((tm, TOP_K * d), row), pl.BlockSpec((None, 1, 6 * d), modrow)]
        args += [y4, modp3]
    in_specs += [
        pl.BlockSpec((None, 1, 6 * d), modrow),
        pl.BlockSpec((1, d), const2),
        pl.BlockSpec((d, IN_EXT), const2),
        pl.BlockSpec((1, MLA_Q_RANK), const2),
        pl.BlockSpec((MLA_Q_RANK, 3 * MLA_HEADS * LANE), const2),
        pl.BlockSpec((1, MLA_KV_RANK), const2),
        pl.BlockSpec((MLA_KV_RANK, 2 * MLA_HEADS * LANE), const2),
        pl.BlockSpec((tm, 2 * LANE), tabrow),
        pl.BlockSpec((tm, 2 * LANE), tabrow),
    ]
    args += [mod3, lw["g1"], lw["w_in"], lw["qn"], lw["w_uq"], lw["kvn"], lw["w_ukv"],
             tabs["mla"], tabs["ret"]]
    out_specs, out_shape = [], []
    if combine:
        out_specs.append(pl.BlockSpec((tm, d), row))
        out_shape.append(jax.ShapeDtypeStruct((nt, d), F32))
    hrow = lambda t: (0, t, 0)
    out_specs += [
        pl.BlockSpec((tm, CONV_CH), row),
        pl.BlockSpec((MLA_HEADS, tm, QK_DIM), hrow),
        pl.BlockSpec((MLA_HEADS, tm, QK_DIM), hrow),
        pl.BlockSpec((MLA_HEADS, tm, MLA_V), hrow),
        pl.BlockSpec((tm, RET_QK), row),
        pl.BlockSpec((tm, RET_QK), row),
        pl.BlockSpec((tm, RET_V), row),
        pl.BlockSpec((tm, RET_V), row),
    ]
    out_shape += [
        jax.ShapeDtypeStruct((nt, CONV_CH), F32),
        jax.ShapeDtypeStruct((MLA_HEADS, nt, QK_DIM), BF16),
        jax.ShapeDtypeStruct((MLA_HEADS, nt, QK_DIM), BF16),
        jax.ShapeDtypeStruct((MLA_HEADS, nt, MLA_V), BF16),
        jax.ShapeDtypeStruct((nt, RET_QK), BF16),
        jax.ShapeDtypeStruct((nt, RET_QK), BF16),
        jax.ShapeDtypeStruct((nt, RET_V), BF16),
        jax.ShapeDtypeStruct((nt, RET_V), F32),
    ]
    return pl.pallas_call(
        functools.partial(_inproj_kernel, combine=combine),
        grid=(nt // tm,),
        in_specs=in_specs, out_specs=out_specs, out_shape=out_shape,
        compiler_params=_cparams(("arbitrary",)),
        name="inproj",
    )(*args)


def _conv_kernel(z_ref, zp_ref, zn_ref, w_ref, b_ref, g_ref, beta_ref, o_ref, e_scr, *, tm, l, c, nl):
    t = pl.program_id(0)
    nsub = tm // CONV_SUB
    rows = 64
    for s in range(nsub):
        g0 = t * tm + s * CONV_SUB
        is_lat = g0 < nl
        pos0 = jnp.where(is_lat, lax.rem(g0, l), lax.rem(g0 - nl, c))
        seqlen = jnp.where(is_lat, l, c)
        first = pos0 == 0
        last = pos0 + CONV_SUB == seqlen
        lo = s * CONV_SUB
        prev = z_ref[lo - CONV_HALO:lo, :] if s > 0 else zp_ref[...]
        nxt = z_ref[lo + CONV_SUB:lo + CONV_SUB + CONV_HALO, :] if s < nsub - 1 else zn_ref[...]
        e_scr[0:CONV_HALO, :] = jnp.where(first, 0.0, prev)
        e_scr[CONV_HALO:CONV_HALO + CONV_SUB, :] = z_ref[lo:lo + CONV_SUB, :]
        e_scr[CONV_HALO + CONV_SUB:2 * CONV_HALO + CONV_SUB, :] = jnp.where(last, 0.0, nxt)
        for r0 in range(0, CONV_SUB, rows):
            acc = jnp.zeros((rows, CONV_CH), F32) + b_ref[...]
            for j in range(CONV_WIDTH):
                off = r0 + j + CONV_HALO - CONV_WIDTH // 2
                acc = acc + w_ref[j:j + 1, :] * e_scr[off:off + rows, :]
            mu = jnp.mean(acc, axis=-1, keepdims=True)
            cen = acc - mu
            var = jnp.mean(cen * cen, axis=-1, keepdims=True)
            y = cen * lax.rsqrt(var + LN_EPS) * g_ref[...] + beta_ref[...]
            o_ref[lo + r0:lo + r0 + rows, :] = (y * jax.nn.sigmoid(y)).astype(o_ref.dtype)


def _conv(z, lw, geo, n_rows):
    tm, l, c, nl, nt = geo["TM"], geo["L"], geo["C"], geo["NL"], geo["NT"]
    hb = tm // CONV_HALO
    nhb = nt // CONV_HALO
    return pl.pallas_call(
        functools.partial(_conv_kernel, tm=tm, l=l, c=c, nl=nl),
        grid=(n_rows // tm,),
        in_specs=[pl.BlockSpec((tm, CONV_CH), lambda t: (t, 0)),
                  pl.BlockSpec((CONV_HALO, CONV_CH), lambda t: (jnp.maximum(t * hb - 1, 0), 0)),
                  pl.BlockSpec((CONV_HALO, CONV_CH), lambda t: (jnp.minimum((t + 1) * hb, nhb - 1), 0)),
                  pl.BlockSpec((CONV_WIDTH, CONV_CH), lambda t: (0, 0)),
                  pl.BlockSpec((1, CONV_CH), lambda t: (0, 0)),
                  pl.BlockSpec((1, CONV_CH), lambda t: (0, 0)),
                  pl.BlockSpec((1, CONV_CH), lambda t: (0, 0))],
        out_specs=pl.BlockSpec((tm, CONV_CH), lambda t: (t, 0)),
        out_shape=jax.ShapeDtypeStruct((n_rows, CONV_CH), BF16),
        scratch_shapes=[pltpu.VMEM((CONV_SUB + 2 * CONV_HALO, CONV_CH), F32)],
        compiler_params=_cparams(("arbitrary",)),
        name="conv_module",
    )(z, z, z, lw["conv_w"], lw["conv_b"], lw["conv_g"], lw["conv_beta"])


def _attn_kernel(*refs, with_latent, tk, n_lat_tiles):
    if with_latent:
        q_ref, kl_ref, vl_ref, kc_ref, vc_ref, o_ref = refs
    else:
        q_ref, kc_ref, vc_ref, o_ref = refs
    q = q_ref[...]
    tq = q.shape[0]

    def step(kt, vt, carry):
        m, lsum, acc = carry
        s = _dot_nt(q, kt)
        m_new = jnp.maximum(m, jnp.max(s, axis=-1, keepdims=True))
        alpha = jnp.exp(m - m_new)
        p = jnp.exp(s - m_new)
        lsum = alpha * lsum + jnp.sum(p, axis=-1, keepdims=True)
        acc = alpha * acc + _dot(p.astype(BF16), vt)
        return m_new, lsum, acc

    carry = (jnp.full((tq, 1), -jnp.inf, F32), jnp.zeros((tq, 1), F32), jnp.zeros((tq, MLA_V), F32))
    if with_latent:
        def body(j, carry):
            off = pl.multiple_of(j * tk, tk)
            return step(kl_ref[pl.ds(off, tk), :], vl_ref[pl.ds(off, tk), :], carry)
        carry = lax.fori_loop(0, n_lat_tiles, body, carry)
    m, lsum, acc = step(kc_ref[...], vc_ref[...], carry)
    o_ref[...] = (acc / lsum).astype(o_ref.dtype)


def _attention_latent(q, k, v, geo):
    b, l, c, nl, nt = geo["B"], geo["L"], geo["C"], geo["NL"], geo["NT"]
    tq, tk = geo["TQ"], geo["TK"]
    nq = l // tq
    cblk = nl // c
    return pl.pallas_call(
        functools.partial(_attn_kernel, with_latent=True, tk=tk, n_lat_tiles=l // tk),
        grid=(b, MLA_HEADS, nq),
        in_specs=[pl.BlockSpec((None, tq, QK_DIM), lambda bi, h, i: (h, bi * nq + i, 0)),
                  pl.BlockSpec((None, l, QK_DIM), lambda bi, h, i: (h, bi, 0)),
                  pl.BlockSpec((None, l, MLA_V), lambda bi, h, i: (h, bi, 0)),
                  pl.BlockSpec((None, c, QK_DIM), lambda bi, h, i: (h, cblk + bi, 0)),
                  pl.BlockSpec((None, c, MLA_V), lambda bi, h, i: (h, cblk + bi, 0))],
        out_specs=pl.BlockSpec((tq, MLA_V), lambda bi, h, i: (bi * nq + i, h)),
        out_shape=jax.ShapeDtypeStruct((nl, MLA_HEADS * MLA_V), BF16),
        compiler_params=_cparams(("arbitrary", "arbitrary", "arbitrary")),
        name="mla_attention",
    )(q, k, v, k, v)


def _attention_context(q, k, v, geo):
    b, c, nl, nc = geo["B"], geo["C"], geo["NL"], geo["NC"]
    cblk = nl // c
    cmap = lambda bi, h: (h, cblk + bi, 0)
    return pl.pallas_call(
        functools.partial(_attn_kernel, with_latent=False, tk=c, n_lat_tiles=0),
        grid=(b, MLA_HEADS),
        in_specs=[pl.BlockSpec((None, c, QK_DIM), cmap),
                  pl.BlockSpec((None, c, QK_DIM), cmap),
                  pl.BlockSpec((None, c, MLA_V), cmap)],
        out_specs=pl.BlockSpec((c, MLA_V), lambda bi, h: (bi, h)),
        out_shape=jax.ShapeDtypeStruct((nc, MLA_HEADS * MLA_V), BF16),
        compiler_params=_cparams(("arbitrary", "arbitrary")),
        name="mla_attention_ctx",
    )(q, k, v)


def _ret_kernel(lg_ref, lgq_ref, lgcol_ref, rql_ref, rkl_ref, rvl_ref, rgl_ref,
                rqc_ref, rkc_ref, rvc_ref, rgc_ref, *rest, l, c, ctx_out):
    if ctx_out:
        yl_ref, yc_ref, sf_scr, m_scr = rest
    else:
        yl_ref, sf_scr, m_scr = rest
        yc_ref = None
    cn = RET_CHUNK
    ncc = c // cn
    ncl = l // cn
    fi = lax.broadcasted_iota(jnp.int32, (cn, 1), 0).astype(F32)
    lgf_q = lgq_ref[0:1, :]
    lgb_q = lgq_ref[1:2, :]
    qdec_f = jnp.exp((fi + 1.0) * lgf_q)
    qdec_b = jnp.exp((float(cn) - fi) * lgb_q)
    kdec_f = jnp.exp((float(cn - 1) - fi) * lgf_q)
    kdec_b = jnp.exp(fi * lgb_q)
    sdec_f = jnp.exp(float(cn) * lgcol_ref[:, 0:1])
    sdec_b = jnp.exp(float(cn) * lgcol_ref[:, 1:2])

    ri = lax.broadcasted_iota(jnp.int32, (cn, cn), 0)
    ci = lax.broadcasted_iota(jnp.int32, (cn, cn), 1)
    diff = (ri - ci).astype(F32)
    for hd in range(RET_HEADS):
        mf = jnp.where(diff >= 0, jnp.exp(jnp.maximum(diff, 0.0) * lg_ref[0, hd]), 0.0)
        mb = jnp.where(diff <= 0, jnp.exp(jnp.maximum(-diff, 0.0) * lg_ref[1, hd]), 0.0)
        m_scr[hd] = mf + mb

    lane_q = lax.broadcasted_iota(jnp.int32, (1, RET_QK), 1) // RET_DK
    lane_v = lax.broadcasted_iota(jnp.int32, (1, RET_V), 1) // RET_DV
    bd = (lax.broadcasted_iota(jnp.int32, (RET_QK, RET_V), 0) // RET_DK
          == lax.broadcasted_iota(jnp.int32, (RET_QK, RET_V), 1) // RET_DV)

    def kv_outer(kf, v, kdec):
        kt = (kf * kdec).T.astype(BF16)
        return jnp.where(bd, _dot(kt, v), 0.0)

    def out_chunk(q, k, v, g, s_f, s_b):
        qf = q.astype(F32)
        ps = []
        for hd in range(RET_HEADS):
            qm = jnp.where(lane_q == hd, q, jnp.zeros_like(q))
            ps.append((_dot_nt(qm, k) * m_scr[hd]).astype(BF16))
        pcat = jnp.concatenate(ps, axis=1)
        vbd = jnp.concatenate([jnp.where(lane_v == hd, v, jnp.zeros_like(v)) for hd in range(RET_HEADS)], axis=0)
        o = _dot(pcat, vbd)
        o = o + _dot((qf * qdec_f).astype(BF16), s_f.astype(BF16))
        o = o + _dot((qf * qdec_b).astype(BF16), s_b.astype(BF16))
        o2 = o * o
        ms = jnp.zeros_like(o)
        for hd in range(RET_HEADS):
            sel = lane_v == hd
            hs = jnp.sum(jnp.where(sel, o2, 0.0), axis=-1, keepdims=True) * (1.0 / RET_DV)
            ms = jnp.where(sel, hs, ms)
        return (g * jax.nn.sigmoid(g)) * (o * lax.rsqrt(ms + RMS_EPS))

    zero_s = jnp.zeros((RET_QK, RET_V), F32)

    s = zero_s
    for cc in range(ncc):
        sf_scr[cc] = s
        sl = slice(cc * cn, (cc + 1) * cn)
        s = s * sdec_f + kv_outer(rkc_ref[sl, :].astype(F32), rvc_ref[sl, :], kdec_f)

    def fwd_body(n, s):
        sf_scr[ncc + n] = s
        off = pl.multiple_of(n * cn, cn)
        return s * sdec_f + kv_outer(rkl_ref[pl.ds(off, cn), :].astype(F32), rvl_ref[pl.ds(off, cn), :], kdec_f)

    lax.fori_loop(0, ncl, fwd_body, s)

    s = zero_s
    for cc in reversed(range(ncc)):
        sl = slice(cc * cn, (cc + 1) * cn)
        k = rkc_ref[sl, :]
        v = rvc_ref[sl, :]
        if ctx_out:
            yc_ref[sl, :] = out_chunk(rqc_ref[sl, :], k, v, rgc_ref[sl, :], sf_scr[cc], s).astype(yc_ref.dtype)
        s = s * sdec_b + kv_outer(k.astype(F32), v, kdec_b)

    def bwd_body(i, s):
        n = ncl - 1 - i
        off = pl.multiple_of(n * cn, cn)
        k = rkl_ref[pl.ds(off, cn), :]
        v = rvl_ref[pl.ds(off, cn), :]
        y = out_chunk(rql_ref[pl.ds(off, cn), :], k, v, rgl_ref[pl.ds(off, cn), :], sf_scr[ncc + n], s)
        yl_ref[pl.ds(off, cn), :] = y.astype(yl_ref.dtype)
        return s * sdec_b + kv_outer(k.astype(F32), v, kdec_b)

    lax.fori_loop(0, ncl, bwd_body, s)


def _retention(rq, rk, rv, rg, lw, geo, ctx_out):
    b, l, c, nl = geo["B"], geo["L"], geo["C"], geo["NL"]
    cblk = nl // c
    lat = lambda w: pl.BlockSpec((l, w), lambda bi: (bi, 0))
    ctx = lambda w: pl.BlockSpec((c, w), lambda bi: (cblk + bi, 0))
    out_specs = [pl.BlockSpec((l, RET_V), lambda bi: (bi, 0))]
    out_shape = [jax.ShapeDtypeStruct((nl, RET_V), BF16)]
    if ctx_out:
        out_specs.append(pl.BlockSpec((c, RET_V), lambda bi: (bi, 0)))
        out_shape.append(jax.ShapeDtypeStruct((b * c, RET_V), BF16))
    nchunks = (l + c) // RET_CHUNK
    return pl.pallas_call(
        functools.partial(_ret_kernel, l=l, c=c, ctx_out=ctx_out),
        grid=(b,),
        in_specs=[pl.BlockSpec(memory_space=pltpu.SMEM),
                  pl.BlockSpec((2, RET_QK), lambda bi: (0, 0)),
                  pl.BlockSpec((RET_QK, 2), lambda bi: (0, 0)),
                  lat(RET_QK), lat(RET_QK), lat(RET_V), lat(RET_V),
                  ctx(RET_QK), ctx(RET_QK), ctx(RET_V), ctx(RET_V)],
        out_specs=out_specs, out_shape=out_shape,
        scratch_shapes=[pltpu.VMEM((nchunks, RET_QK, RET_V), F32),
                        pltpu.VMEM((RET_HEADS, RET_CHUNK, RET_CHUNK), F32)],
        compiler_params=_cparams(("arbitrary",)),
        name="retention",
    )(lw["ret_lg"], lw["ret_lg_lane"], lw["ret_lg_col"], rq, rk, rv, rg, rq, rk, rv, rg)


def _outproj_kernel(yc_ref, oml_ref, omc_ref, yrl_ref, yrc_ref, x_ref, mod_ref, w0_ref, w1_ref, w2_ref, g2_ref,
                    rw_ref, rb_ref, xo_ref, h2_ref, idx_ref, gate_ref, *, nlt):
    d = D_MODEL
    is_lat = pl.program_id(0) < nlt
    om = jnp.where(is_lat, oml_ref[...], omc_ref[...])
    yr = jnp.where(is_lat, yrl_ref[...], yrc_ref[...])
    mix = _dot(yc_ref[...], w0_ref[...]) + _dot(om, w1_ref[...]) + _dot(yr, w2_ref[...])
    x = x_ref[...] + mod_ref[:, 2 * d:3 * d] * mix
    xo_ref[...] = x
    h2 = _rms(x) * g2_ref[...] * (1.0 + mod_ref[:, 4 * d:5 * d]) + mod_ref[:, 3 * d:4 * d]
    h2_ref[...] = h2
    lg = _dot(h2, rw_ref[...], precision=lax.Precision.HIGHEST) + rb_ref[...]
    lane = lax.broadcasted_iota(jnp.int32, lg.shape, 1)
    idx_out = jnp.zeros(lg.shape, jnp.int32)
    gate_out = jnp.zeros(lg.shape, F32)
    v0 = None
    den = None
    es = []
    for kk in range(TOP_K):
        m = jnp.max(lg, axis=-1, keepdims=True)
        idx = jnp.min(jnp.where(lg == m, lane, N_EXPERTS - 1), axis=-1, keepdims=True)
        lg = jnp.where(lane == idx, -jnp.inf, lg)
        if kk == 0:
            v0 = m
        e = jnp.exp(m - v0)
        es.append(e)
        den = e if den is None else den + e
        idx_out = jnp.where(lane == kk, idx, idx_out)
    for kk in range(TOP_K):
        gate_out = jnp.where(lane == kk, es[kk] / den, gate_out)
    idx_ref[...] = idx_out
    gate_ref[...] = gate_out


def _outproj(yc, om_l, om_c, yr_l, yr_c, x, mod3, lw, geo, n_rows):
    tm, nlt, l, b = geo["TM"], geo["NLT"], geo["L"], geo["B"]
    d = D_MODEL
    row = lambda t: (t, 0)
    latrow = lambda t: (jnp.minimum(t, nlt - 1), 0)
    ctxrow = lambda t: (jnp.maximum(t - nlt, 0), 0)
    const2 = lambda t: (0, 0)
    modrow = lambda t: (jnp.where(t < nlt, (t * tm) // l, b), 0, 0)
    return pl.pallas_call(
        functools.partial(_outproj_kernel, nlt=nlt),
        grid=(n_rows // tm,),
        in_specs=[pl.BlockSpec((tm, CONV_CH), row),
                  pl.BlockSpec((tm, MLA_HEADS * MLA_V), latrow),
                  pl.BlockSpec((tm, MLA_HEADS * MLA_V), ctxrow),
                  pl.BlockSpec((tm, RET_V), latrow),
                  pl.BlockSpec((tm, RET_V), ctxrow),
                  pl.BlockSpec((tm, d), row),
                  pl.BlockSpec((None, 1, 6 * d), modrow),
                  pl.BlockSpec((CONV_CH, d), const2),
                  pl.BlockSpec((MLA_HEADS * MLA_V, d), const2),
                  pl.BlockSpec((RET_V, d), const2),
                  pl.BlockSpec((1, d), const2),
                  pl.BlockSpec((d, LANE), const2),
                  pl.BlockSpec((1, LANE), const2)],
        out_specs=[pl.BlockSpec((tm, d), row), pl.BlockSpec((tm, d), row),
                   pl.BlockSpec((tm, LANE), row), pl.BlockSpec((tm, LANE), row)],
        out_shape=[jax.ShapeDtypeStruct((n_rows, d), F32), jax.ShapeDtypeStruct((n_rows, d), F32),
                   jax.ShapeDtypeStruct((n_rows, LANE), jnp.int32), jax.ShapeDtypeStruct((n_rows, LANE), F32)],
        compiler_params=_cparams(("arbitrary",)),
        name="outproj_router",
    )(yc, om_l, om_c, yr_l, yr_c, x, mod3, lw["w_out0"], lw["w_out1"], lw["w_out2"], lw["g2"], lw["router_w"], lw["router_b"])


def _moe_kernel(be_ref, nreal_ref, src_ref, dst_ref, gate_ref, h2_hbm, w1_ref, b1_ref, w2_ref, b2_ref,
                y4_hbm, xbuf, ybuf, sem_g, sem_s, *, bm):
    del be_ref
    i = pl.program_id(0)

    def gather_copy(r, tok):
        return pltpu.make_async_copy(h2_hbm.at[pl.ds(tok, 1), :], xbuf.at[pl.ds(r, 1), :], sem_g)

    def scatter_copy(r, row):
        return pltpu.make_async_copy(ybuf.at[pl.ds(r, 1), :], y4_hbm.at[pl.ds(row, 1), :], sem_s)

    @pl.when(i < nreal_ref[0])
    def _():
        def g_start(r, carry):
            gather_copy(r, src_ref[0, r]).start()
            return carry
        lax.fori_loop(0, bm, g_start, 0)

        def g_wait(r, carry):
            gather_copy(r, 0).wait()
            return carry
        lax.fori_loop(0, bm, g_wait, 0)

        x = xbuf[...].astype(BF16)
        hid = _dot(x, w1_ref[...]) + b1_ref[...]
        glu = jnp.minimum(hid[:, :D_FF], SWIGLU_LIMIT)
        lin = jnp.clip(hid[:, D_FF:], -SWIGLU_LIMIT, SWIGLU_LIMIT)
        act = glu * jax.nn.sigmoid(SWIGLU_ALPHA * glu) * (lin + 1.0)
        out = (_dot(act.astype(BF16), w2_ref[...]) + b2_ref[...]) * gate_ref[...]
        ybuf[...] = out

        def s_start(r, carry):
            scatter_copy(r, dst_ref[0, r]).start()
            return carry
        lax.fori_loop(0, bm, s_start, 0)

        def s_wait(r, carry):
            scatter_copy(r, 0).wait()
            return carry
        lax.fori_loop(0, bm, s_wait, 0)


def _moe(h2, top_idx, gates, lw, geo):
    n = h2.shape[0]
    d = D_MODEL
    bm = geo["BM"]
    a = n * TOP_K
    n_blocks = a // bm + N_EXPERTS
    n_slots = n_blocks * bm

    e_flat = top_idx.reshape(a)
    order = jnp.argsort(e_flat).astype(jnp.int32)
    counts = jnp.bincount(e_flat, length=N_EXPERTS).astype(jnp.int32)
    padded = (counts + bm - 1) // bm * bm
    padded_end = jnp.cumsum(padded)
    padded_start = padded_end - padded
    group_start = jnp.cumsum(counts) - counts
    blk = jnp.arange(n_blocks, dtype=jnp.int32)
    block_expert = jnp.minimum(jnp.searchsorted(padded_end, blk * bm, side="right"), N_EXPERTS - 1).astype(jnp.int32)
    n_real = (padded_end[-1] // bm).astype(jnp.int32).reshape(1)
    slot = jnp.arange(n_slots, dtype=jnp.int32)
    slot_e = jnp.repeat(block_expert, bm)
    rank = slot - padded_start[slot_e]
    valid = (rank < counts[slot_e]) & (slot < padded_end[-1])
    sp = jnp.clip(group_start[slot_e] + rank, 0, a - 1)
    assign = order[sp]
    slot_src = jnp.where(valid, assign // TOP_K, 0).astype(jnp.int32)
    slot_dst = jnp.where(valid, assign, a + slot % bm).astype(jnp.int32)
    slot_gate = jnp.where(valid, gates.reshape(a)[assign], 0.0).astype(F32)

    smem_blk = pl.BlockSpec((None, 1, bm), lambda i, be, nr: (i, 0, 0), memory_space=pltpu.SMEM)
    y4 = pl.pallas_call(
        functools.partial(_moe_kernel, bm=bm),
        grid_spec=pltpu.PrefetchScalarGridSpec(
            num_scalar_prefetch=2,
            grid=(n_blocks,),
            in_specs=[smem_blk, smem_blk,
                      pl.BlockSpec((None, bm, 1), lambda i, be, nr: (i, 0, 0)),
                      pl.BlockSpec(memory_space=pl.ANY),
                      pl.BlockSpec((None, d, 2 * D_FF), lambda i, be, nr: (be[i], 0, 0)),
                      pl.BlockSpec((None, 1, 2 * D_FF), lambda i, be, nr: (be[i], 0, 0)),
                      pl.BlockSpec((None, D_FF, d), lambda i, be, nr: (be[i], 0, 0)),
                      pl.BlockSpec((None, 1, d), lambda i, be, nr: (be[i], 0, 0))],
            out_specs=pl.BlockSpec(memory_space=pl.ANY),
            scratch_shapes=[pltpu.VMEM((bm, d), F32), pltpu.VMEM((bm, d), F32),
                            pltpu.SemaphoreType.DMA(()), pltpu.SemaphoreType.DMA(())]),
        out_shape=jax.ShapeDtypeStruct((a + bm, d), F32),
        compiler_params=_cparams(("arbitrary",)),
        name="moe_experts",
    )(block_expert, n_real, slot_src.reshape(n_blocks, 1, bm), slot_dst.reshape(n_blocks, 1, bm),
      slot_gate.reshape(n_blocks, bm, 1), h2, lw["moe_w1"], lw["moe_b1"], lw["moe_w2"], lw["moe_b2"])
    return y4.reshape(n + bm // TOP_K, TOP_K * d)


def _final_kernel(x_ref, y4_ref, mod_ref, g_ref, o_ref):
    d = D_MODEL
    y = (y4_ref[:, 0:d] + y4_ref[:, d:2 * d]) + (y4_ref[:, 2 * d:3 * d] + y4_ref[:, 3 * d:4 * d])
    x = x_ref[...] + mod_ref[:, 5 * d:6 * d] * y
    o_ref[...] = _rms(x) * g_ref[...]


def _final(x, y4, mod3, g_final, geo):
    tm, l, nl = geo["TM"], geo["L"], geo["NL"]
    d = D_MODEL
    row = lambda t: (t, 0)
    return pl.pallas_call(
        _final_kernel,
        grid=(nl // tm,),
        in_specs=[pl.BlockSpec((tm, d), row), pl.BlockSpec((tm, TOP_K * d), row),
                  pl.BlockSpec((None, 1, 6 * d), lambda t: ((t * tm) // l, 0, 0)),
                  pl.BlockSpec((1, d), lambda t: (0, 0))],
        out_specs=pl.BlockSpec((tm, d), row),
        out_shape=jax.ShapeDtypeStruct((nl, d), F32),
        compiler_params=_cparams(("arbitrary",)),
        name="final_norm",
    )(x, y4, mod3, g_final)


def _rot_cols(w, half):
    return jnp.concatenate([-w[:, half:2 * half], w[:, :half]], axis=1)


def _pack_layer(p, li):
    d = D_MODEL
    w_in = p["w_in"][li]
    o = 0
    a = w_in[:, 0:256]
    g = w_in[:, 256:512]
    cq = w_in[:, 512:896]
    rq = w_in[:, 896:1024]
    rg = w_in[:, 1024:1280]
    ckv = w_in[:, 1280:1536]
    kr = w_in[:, 1536:1600]
    rk = w_in[:, 1600:1728]
    rv = w_in[:, 1728:1984]
    del o

    def head_rot(w):
        return jnp.concatenate([_rot_cols(w[:, h * RET_DK:(h + 1) * RET_DK], RET_DK // 2)
                                for h in range(RET_HEADS)], axis=1)

    zpad = jnp.zeros((d, LANE - MLA_ROPE), F32)
    w_in_ext = jnp.concatenate(
        [a, g, cq, rq, head_rot(rq), rg, ckv, rk, head_rot(rk), rv,
         kr, zpad, _rot_cols(kr, MLA_ROPE // 2), zpad], axis=1).astype(BF16)

    w_uq = p["mla_w_uq"][li]
    hq = MLA_NOPE + MLA_ROPE
    zq = jnp.zeros((MLA_Q_RANK, LANE - MLA_ROPE), F32)
    wn = [w_uq[:, h * hq:h * hq + MLA_NOPE] for h in range(MLA_HEADS)]
    wr = [jnp.concatenate([w_uq[:, h * hq + MLA_NOPE:(h + 1) * hq], zq], axis=1) for h in range(MLA_HEADS)]
    wrt = [jnp.concatenate([_rot_cols(w_uq[:, h * hq + MLA_NOPE:(h + 1) * hq], MLA_ROPE // 2), zq], axis=1)
           for h in range(MLA_HEADS)]
    w_uq_ext = jnp.concatenate(wn + wr + wrt, axis=1).astype(BF16)

    w_ukv = p["mla_w_ukv"][li]
    hk = MLA_NOPE + MLA_V
    wk = [w_ukv[:, h * hk:h * hk + MLA_NOPE] for h in range(MLA_HEADS)]
    wv = [w_ukv[:, h * hk + MLA_NOPE:(h + 1) * hk] for h in range(MLA_HEADS)]
    w_ukv_ext = jnp.concatenate(wk + wv, axis=1).astype(BF16)

    lgf = jax.nn.log_sigmoid(p["ret_decay_fwd"][li].astype(F32))
    lgb = jax.nn.log_sigmoid(p["ret_decay_bwd"][li].astype(F32))
    lg = jnp.stack([lgf, lgb])
    lg_lane = jnp.repeat(lg, RET_DK, axis=1)

    w_out = p["w_out"][li].astype(BF16)
    rw = jnp.zeros((d, LANE), F32).at[:, :N_EXPERTS].set(p["router_w"][li])
    rb = jnp.full((1, LANE), NEG_BIG, F32).at[0, :N_EXPERTS].set(p["router_b"][li].astype(F32))
    return {
        "g1": p["g_norm1"][li].reshape(1, d), "g2": p["g_norm2"][li].reshape(1, d),
        "w_in": w_in_ext, "qn": p["mla_q_norm"][li].reshape(1, -1), "w_uq": w_uq_ext,
        "kvn": p["mla_kv_norm"][li].reshape(1, -1), "w_ukv": w_ukv_ext,
        "conv_w": p["conv_w"][li], "conv_b": p["conv_b"][li].reshape(1, -1),
        "conv_g": p["conv_ln_g"][li].reshape(1, -1), "conv_beta": p["conv_ln_b"][li].reshape(1, -1),
        "ret_lg": lg, "ret_lg_lane": lg_lane, "ret_lg_col": lg_lane.T,
        "w_out0": w_out[0:CONV_CH], "w_out1": w_out[CONV_CH:CONV_CH + MLA_HEADS * MLA_V],
        "w_out2": w_out[CONV_CH + MLA_HEADS * MLA_V:],
        "router_w": rw, "router_b": rb,
        "moe_w1": p["moe_w1"][li].astype(BF16), "moe_b1": p["moe_b1"][li].reshape(N_EXPERTS, 1, -1),
        "moe_w2": p["moe_w2"][li].astype(BF16), "moe_b2": p["moe_b2"][li].reshape(N_EXPERTS, 1, -1),
    }


def _position_tables(l, c, tm):
    rows = l // GRID_W
    row = jnp.repeat(jnp.arange(rows), GRID_W).astype(F32)
    col = jnp.tile(jnp.arange(GRID_W), rows).astype(F32)
    npa = MLA_ROPE // 4
    freq = ROPE_BASE ** (-jnp.arange(npa, dtype=F32) / npa)
    ang = jnp.concatenate([row[:, None] * freq, col[:, None] * freq], axis=-1)
    zl = jnp.zeros((l, LANE - MLA_ROPE), F32)
    mla_lat = jnp.concatenate([jnp.cos(ang), jnp.cos(ang), zl, jnp.sin(ang), jnp.sin(ang), zl], axis=1)
    mla_ctx = jnp.concatenate([jnp.ones((tm, LANE), F32), jnp.zeros((tm, LANE), F32)], axis=1)
    mla = jnp.concatenate([mla_lat, mla_ctx], axis=0)

    theta = 1.0 / (ROPE_BASE ** jnp.linspace(0.0, 1.0, RET_DK // 2, dtype=F32))

    def ret_tab(pos):
        a = pos.astype(F32)[:, None] * theta
        cs = jnp.tile(jnp.concatenate([jnp.cos(a), jnp.cos(a)], axis=1), (1, RET_HEADS))
        sn = jnp.tile(jnp.concatenate([jnp.sin(a), jnp.sin(a)], axis=1), (1, RET_HEADS))
        return jnp.concatenate([cs, sn], axis=1)

    ret = jnp.concatenate([ret_tab(c + jnp.arange(l)), jnp.tile(ret_tab(jnp.arange(c)), (tm // c, 1))], axis=0)
    return {"mla": mla, "ret": ret}


def kernel(x, c, ctx, c_ctx, w_ada, b_ada, g_norm1, g_norm2, w_in, w_out, conv_w, conv_b, conv_ln_g, conv_ln_b,
           mla_q_norm, mla_w_uq, mla_kv_norm, mla_w_ukv, ret_decay_fwd, ret_decay_bwd, router_w, router_b,
           moe_w1, moe_b1, moe_w2, moe_b2, g_final):
    b, l, d = x.shape
    cl = ctx.shape[1]
    depth = w_ada.shape[0]
    tm = 512
    nl, nc = b * l, b * cl
    nt = nl + nc
    assert d == D_MODEL and l % tm == 0 and nc % tm == 0 and tm % cl == 0 and cl % CONV_SUB == 0
    geo = {"B": b, "L": l, "C": cl, "NL": nl, "NC": nc, "NT": nt, "TM": tm, "NLT": nl // tm, "LT": l // tm,
           "TQ": min(512, l), "TK": min(512, l), "BM": 256}
    p = {"w_in": w_in, "w_out": w_out, "g_norm1": g_norm1, "g_norm2": g_norm2, "conv_w": conv_w, "conv_b": conv_b,
         "conv_ln_g": conv_ln_g, "conv_ln_b": conv_ln_b, "mla_q_norm": mla_q_norm, "mla_w_uq": mla_w_uq,
         "mla_kv_norm": mla_kv_norm, "mla_w_ukv": mla_w_ukv, "ret_decay_fwd": ret_decay_fwd,
         "ret_decay_bwd": ret_decay_bwd, "router_w": router_w, "router_b": router_b, "moe_w1": moe_w1,
         "moe_b1": moe_b1, "moe_w2": moe_w2, "moe_b2": moe_b2}

    r = -(-(b + 1) // 8) * 8
    cs = jnp.zeros((r, d), F32).at[:b].set(c).at[b].set(c_ctx)
    mod = _modulation(cs, w_ada, b_ada)
    tabs = _position_tables(l, cl, tm)

    xa = jnp.concatenate([x.reshape(nl, d), ctx.reshape(nc, d)], axis=0)
    y4 = None
    for li in range(depth):
        lw = _pack_layer(p, li)
        mod3 = mod[li].reshape(r, 1, 6 * d)
        update_ctx = li < depth - 1
        if y4 is None:
            z, q, k, v, rq, rk, rv, rg = _inproj(xa, mod3, lw, tabs, geo)
        else:
            xa, z, q, k, v, rq, rk, rv, rg = _inproj(xa, mod3, lw, tabs, geo, y4=y4,
                                                     modp3=mod[li - 1].reshape(r, 1, 6 * d))
        n_rows = nt if update_ctx else nl
        yc = _conv(z, lw, geo, n_rows)
        om_l = _attention_latent(q, k, v, geo)
        if update_ctx:
            om_c = _attention_context(q, k, v, geo)
            yr_l, yr_c = _retention(rq, rk, rv, rg, lw, geo, True)
        else:
            (yr_l,) = _retention(rq, rk, rv, rg, lw, geo, False)
            om_c, yr_c = om_l, yr_l
        xa, h2, idx, gates = _outproj(yc, om_l, om_c, yr_l, yr_c, xa, mod3, lw, geo, n_rows)
        y4 = _moe(h2, idx[:, :TOP_K], gates[:, :TOP_K], lw, geo)
    out = _final(xa, y4, mod[depth - 1].reshape(r, 1, 6 * d), g_final.reshape(1, d), geo)
    return out.reshape(b, l, d)
```

```python
import functools

import jax
import jax.numpy as jnp
from jax import lax
from jax.experimental import pallas as pl
from jax.experimental.pallas import tpu as pltpu

F32 = jnp.float32
BF16 = jnp.bfloat16

D_MODEL = 1024
CONV_CH = 256
CONV_WIDTH = 31
MLA_HEADS = 4
MLA_NOPE = 128
MLA_ROPE = 64
MLA_V = 128
MLA_Q_RANK = 384
MLA_KV_RANK = 256
RET_HEADS = 4
RET_DK = 32
RET_DV = 64
N_EXPERTS = 32
TOP_K = 4
D_FF = 1024
SWIGLU_LIMIT = 7.0
SWIGLU_ALPHA = 1.702
GRID_W = 64
ROPE_BASE = 10000.0
RMS_EPS = 1e-6
LN_EPS = 1e-5

LANE = 128
QK_DIM = 2 * LANE
V_EXT = 2 * LANE
ATT_ROWS = 64
LOG2_E = 1.4426950408889634
RET_QK = RET_HEADS * RET_DK
RET_V = RET_HEADS * RET_DV
CONV_HALO = 16
CONV_SUB = 256
RET_CHUNK = 128
NEG_BIG = -1e30

_O_A, _O_G, _O_CQ, _O_RQ, _O_RQT, _O_RG, _O_CKV, _O_RK, _O_RKT, _O_RV, _O_KR, _O_KRT, IN_EXT = (
    0, 256, 512, 896, 1024, 1152, 1408, 1664, 1792, 1920, 2176, 2304, 2432)

VMEM_LIMIT = 56 * 1024 * 1024


def _cparams(sem):
    return pltpu.CompilerParams(dimension_semantics=sem, vmem_limit_bytes=VMEM_LIMIT)


def _rms(x, eps=RMS_EPS):
    return x * lax.rsqrt(jnp.mean(x * x, axis=-1, keepdims=True) + eps)


def _dot(a, b, **kw):
    return jnp.dot(a, b, preferred_element_type=F32, **kw)


def _dot_nt(a, b):
    return lax.dot_general(a, b, (((1,), (1,)), ((), ())), preferred_element_type=F32)


def _mod_kernel(cs_ref, w_ref, b_ref, o_ref):
    s = cs_ref[...]
    s = s * jax.nn.sigmoid(s)
    o_ref[...] = _dot(s, w_ref[...], precision=lax.Precision.HIGHEST) + b_ref[...]


def _modulation(cs, w_ada, b_ada):
    depth, d, n = w_ada.shape
    r = cs.shape[0]
    tn = 1536
    return pl.pallas_call(
        _mod_kernel,
        grid=(depth, n // tn),
        in_specs=[pl.BlockSpec((r, d), lambda l, j: (0, 0)),
                  pl.BlockSpec((None, d, tn), lambda l, j: (l, 0, j)),
                  pl.BlockSpec((None, 1, tn), lambda l, j: (l, 0, j))],
        out_specs=pl.BlockSpec((None, r, tn), lambda l, j: (l, 0, j)),
        out_shape=jax.ShapeDtypeStruct((depth, r, n), F32),
        compiler_params=_cparams(("arbitrary", "arbitrary")),
        name="modulation",
    )(cs, w_ada, b_ada.reshape(depth, 1, n))


def _inproj_kernel(*refs, combine):
    if combine:
        (x_ref, ya_ref, yb_ref, yc_ref, yd_ref, modp_ref, mod_ref, g1_ref, win_ref, qn_ref, wuq_ref, kvn_ref,
         wukv_ref, mcs_ref, rcs_ref, xo_ref, z_ref, q_ref, k_ref, v_ref, rq_ref, rk_ref, rv_ref, rg_ref) = refs
    else:
        (x_ref, mod_ref, g1_ref, win_ref, qn_ref, wuq_ref, kvn_ref, wukv_ref,
         mcs_ref, rcs_ref, z_ref, q_ref, k_ref, v_ref, rq_ref, rk_ref, rv_ref, rg_ref) = refs
    d = D_MODEL
    x = x_ref[...]
    if combine:
        y = (ya_ref[...] + yb_ref[...]) + (yc_ref[...] + yd_ref[...])
        x = x + modp_ref[:, 5 * d:6 * d] * y
        xo_ref[...] = x
    h = _rms(x) * g1_ref[...] * (1.0 + mod_ref[:, d:2 * d]) + mod_ref[:, 0:d]
    u = _dot(h.astype(BF16), win_ref[...])

    z_ref[...] = u[:, _O_A:_O_G] * jax.nn.sigmoid(u[:, _O_G:_O_CQ])

    mc = mcs_ref[:, 0:LANE]
    ms = mcs_ref[:, LANE:2 * LANE]
    rc = rcs_ref[:, 0:LANE]
    rs = rcs_ref[:, LANE:2 * LANE]

    cqn = _rms(u[:, _O_CQ:_O_RQ]) * qn_ref[...]
    uq = _dot(cqn.astype(BF16), wuq_ref[...])
    scale = float(MLA_NOPE + MLA_ROPE) ** -0.5 * LOG2_E
    hw = MLA_HEADS * LANE
    for hd in range(MLA_HEADS):
        lo = hd * LANE
        q_ref[hd, :, 0:LANE] = (uq[:, lo:lo + LANE] * scale).astype(BF16)
        rope = uq[:, hw + lo:hw + lo + LANE] * mc + uq[:, 2 * hw + lo:2 * hw + lo + LANE] * ms
        q_ref[hd, :, LANE:2 * LANE] = (rope * scale).astype(BF16)

    ckvn = _rms(u[:, _O_CKV:_O_RK]) * kvn_ref[...]
    ukv = _dot(ckvn.astype(BF16), wukv_ref[...])
    krope = (u[:, _O_KR:_O_KRT] * mc + u[:, _O_KRT:IN_EXT] * ms).astype(BF16)
    ones_col = (lax.broadcasted_iota(jnp.int32, krope.shape, 1) == 0).astype(BF16)
    for hd in range(MLA_HEADS):
        lo = hd * LANE
        k_ref[hd, :, 0:LANE] = ukv[:, lo:lo + LANE].astype(BF16)
        k_ref[hd, :, LANE:2 * LANE] = krope
        v_ref[hd, :, 0:LANE] = ukv[:, hw + lo:hw + lo + LANE].astype(BF16)
        v_ref[hd, :, LANE:2 * LANE] = ones_col

    rq_ref[...] = (u[:, _O_RQ:_O_RQT] * rc + u[:, _O_RQT:_O_RG] * rs).astype(BF16)
    rk_ref[...] = ((u[:, _O_RK:_O_RKT] * rc + u[:, _O_RKT:_O_RV] * rs) * (float(RET_DK) ** -0.5)).astype(BF16)
    rv_ref[...] = u[:, _O_RV:_O_KR].astype(BF16)
    rg_ref[...] = u[:, _O_RG:_O_CKV]


def _y4_specs(tm, n):
    return [pl.BlockSpec((tm, D_MODEL), functools.partial(lambda t, kk: (kk * (n // tm) + t, 0), kk=kk))
            for kk in range(TOP_K)]


def _inproj(x, mod3, lw, tabs, geo, y4=None, modp3=None):
    nt, tm, nlt, lt = geo["NT"], geo["TM"], geo["NLT"], geo["LT"]
    b, l = geo["B"], geo["L"]
    d = D_MODEL
    combine = y4 is not None
    row = lambda t: (t, 0)
    const2 = lambda t: (0, 0)
    modrow = lambda t: (jnp.where(t < nlt, (t * tm) // l, b), 0, 0)
    tabrow = lambda t: (jnp.where(t < nlt, t % lt, lt), 0)
    in_specs = [pl.BlockSpec((tm, d), row)]
    args = [x]
    if combine:
        in_specs += _y4_specs(tm, nt) + [pl.BlockSpec((None, 1, 6 * d), modrow)]
        args += [y4] * TOP_K + [modp3]
    in_specs += [
        pl.BlockSpec((None, 1, 6 * d), modrow),
        pl.BlockSpec((1, d), const2),
        pl.BlockSpec((d, IN_EXT), const2),
        pl.BlockSpec((1, MLA_Q_RANK), const2),
        pl.BlockSpec((MLA_Q_RANK, 3 * MLA_HEADS * LANE), const2),
        pl.BlockSpec((1, MLA_KV_RANK), const2),
        pl.BlockSpec((MLA_KV_RANK, 2 * MLA_HEADS * LANE), const2),
        pl.BlockSpec((tm, 2 * LANE), tabrow),
        pl.BlockSpec((tm, 2 * LANE), tabrow),
    ]
    args += [mod3, lw["g1"], lw["w_in"], lw["qn"], lw["w_uq"], lw["kvn"], lw["w_ukv"],
             tabs["mla"], tabs["ret"]]
    out_specs, out_shape = [], []
    if combine:
        out_specs.append(pl.BlockSpec((tm, d), row))
        out_shape.append(jax.ShapeDtypeStruct((nt, d), F32))
    hrow = lambda t: (0, t, 0)
    out_specs += [
        pl.BlockSpec((tm, CONV_CH), row),
        pl.BlockSpec((MLA_HEADS, tm, QK_DIM), hrow),
        pl.BlockSpec((MLA_HEADS, tm, QK_DIM), hrow),
        pl.BlockSpec((MLA_HEADS, tm, V_EXT), hrow),
        pl.BlockSpec((tm, RET_QK), row),
        pl.BlockSpec((tm, RET_QK), row),
        pl.BlockSpec((tm, RET_V), row),
        pl.BlockSpec((tm, RET_V), row),
    ]
    out_shape += [
        jax.ShapeDtypeStruct((nt, CONV_CH), F32),
        jax.ShapeDtypeStruct((MLA_HEADS, nt, QK_DIM), BF16),
        jax.ShapeDtypeStruct((MLA_HEADS, nt, QK_DIM), BF16),
        jax.ShapeDtypeStruct((MLA_HEADS, nt, V_EXT), BF16),
        jax.ShapeDtypeStruct((nt, RET_QK), BF16),
        jax.ShapeDtypeStruct((nt, RET_QK), BF16),
        jax.ShapeDtypeStruct((nt, RET_V), BF16),
        jax.ShapeDtypeStruct((nt, RET_V), F32),
    ]
    return pl.pallas_call(
        functools.partial(_inproj_kernel, combine=combine),
        grid=(nt // tm,),
        in_specs=in_specs, out_specs=out_specs, out_shape=out_shape,
        compiler_params=_cparams(("arbitrary",)),
        name="inproj",
    )(*args)


def _conv_kernel(z_ref, zp_ref, zn_ref, w_ref, b_ref, g_ref, beta_ref, o_ref, e_scr, *, tm, l, c, nl):
    t = pl.program_id(0)
    nsub = tm // CONV_SUB
    rows = 64
    for s in range(nsub):
        g0 = t * tm + s * CONV_SUB
        is_lat = g0 < nl
        pos0 = jnp.where(is_lat, lax.rem(g0, l), lax.rem(g0 - nl, c))
        seqlen = jnp.where(is_lat, l, c)
        first = pos0 == 0
        last = pos0 + CONV_SUB == seqlen
        lo = s * CONV_SUB
        prev = z_ref[lo - CONV_HALO:lo, :] if s > 0 else zp_ref[...]
        nxt = z_ref[lo + CONV_SUB:lo + CONV_SUB + CONV_HALO, :] if s < nsub - 1 else zn_ref[...]
        e_scr[0:CONV_HALO, :] = jnp.where(first, 0.0, prev)
        e_scr[CONV_HALO:CONV_HALO + CONV_SUB, :] = z_ref[lo:lo + CONV_SUB, :]
        e_scr[CONV_HALO + CONV_SUB:2 * CONV_HALO + CONV_SUB, :] = jnp.where(last, 0.0, nxt)
        for r0 in range(0, CONV_SUB, rows):
            acc = jnp.zeros((rows, CONV_CH), F32) + b_ref[...]
            for j in range(CONV_WIDTH):
                off = r0 + j + CONV_HALO - CONV_WIDTH // 2
                acc = acc + w_ref[j:j + 1, :] * e_scr[off:off + rows, :]
            mu = jnp.mean(acc, axis=-1, keepdims=True)
            cen = acc - mu
            var = jnp.mean(cen * cen, axis=-1, keepdims=True)
            y = cen * lax.rsqrt(var + LN_EPS) * g_ref[...] + beta_ref[...]
            o_ref[lo + r0:lo + r0 + rows, :] = (y * jax.nn.sigmoid(y)).astype(o_ref.dtype)


def _conv(z, lw, geo, n_rows):
    tm, l, c, nl, nt = geo["TM"], geo["L"], geo["C"], geo["NL"], geo["NT"]
    hb = tm // CONV_HALO
    nhb = nt // CONV_HALO
    return pl.pallas_call(
        functools.partial(_conv_kernel, tm=tm, l=l, c=c, nl=nl),
        grid=(n_rows // tm,),
        in_specs=[pl.BlockSpec((tm, CONV_CH), lambda t: (t, 0)),
                  pl.BlockSpec((CONV_HALO, CONV_CH), lambda t: (jnp.maximum(t * hb - 1, 0), 0)),
                  pl.BlockSpec((CONV_HALO, CONV_CH), lambda t: (jnp.minimum((t + 1) * hb, nhb - 1), 0)),
                  pl.BlockSpec((CONV_WIDTH, CONV_CH), lambda t: (0, 0)),
                  pl.BlockSpec((1, CONV_CH), lambda t: (0, 0)),
                  pl.BlockSpec((1, CONV_CH), lambda t: (0, 0)),
                  pl.BlockSpec((1, CONV_CH), lambda t: (0, 0))],
        out_specs=pl.BlockSpec((tm, CONV_CH), lambda t: (t, 0)),
        out_shape=jax.ShapeDtypeStruct((n_rows, CONV_CH), BF16),
        scratch_shapes=[pltpu.VMEM((CONV_SUB + 2 * CONV_HALO, CONV_CH), F32)],
        compiler_params=_cparams(("arbitrary",)),
        name="conv_module",
    )(z, z, z, lw["conv_w"], lw["conv_b"], lw["conv_g"], lw["conv_beta"])


def _attn_kernel(*refs, with_latent, tk, n_lat_tiles):
    if with_latent:
        q_ref, kl_ref, vl_ref, kc_ref, vc_ref, o_ref, s0, s1, p0, p1, a0, a1, acc_scr, m_scr = refs
    else:
        q_ref, kc_ref, vc_ref, o_ref, s0, s1, p0, p1, a0, a1, acc_scr, m_scr = refs
    s_b, p_b, a_b = (s0, s1), (p0, p1), (a0, a1)
    tq = q_ref.shape[0]
    cw = kc_ref.shape[0]
    m_scr[...] = jnp.full(m_scr.shape, -jnp.inf, F32)
    acc_scr[...] = jnp.zeros(acc_scr.shape, F32)

    def scores(slot, k, width):
        s_b[slot][:, 0:width] = _dot_nt(q_ref[...], k)

    def softmax(slot, width):
        for r0 in range(0, tq, ATT_ROWS):
            rows = slice(r0, r0 + ATT_ROWS)
            s = s_b[slot][rows, 0:width]
            m_old = m_scr[rows, :]
            m_new = jnp.maximum(m_old, jnp.max(s, axis=-1, keepdims=True))
            a_b[slot][rows, :] = jnp.exp2(m_old - m_new)
            m_scr[rows, :] = m_new
            p_b[slot][rows, 0:width] = jnp.exp2(s - m_new).astype(BF16)

    def pv(slot, v, width):
        acc_scr[...] = acc_scr[...] * a_b[slot][...] + _dot(p_b[slot][:, 0:width], v)

    def k_tile(t):
        return kl_ref[pl.ds(pl.multiple_of(t * tk, tk), tk), :]

    def v_tile(t):
        return vl_ref[pl.ds(pl.multiple_of(t * tk, tk), tk), :]

    if with_latent:
        n = n_lat_tiles
        scores(0, kl_ref[0:tk, :], tk)
        scores(1, kl_ref[tk:2 * tk, :], tk)
        softmax(0, tk)

        def body(jj, carry):
            t = 2 * jj
            scores(0, k_tile(t + 2), tk)
            softmax(1, tk)
            pv(0, v_tile(t), tk)
            scores(1, k_tile(t + 3), tk)
            softmax(0, tk)
            pv(1, v_tile(t + 1), tk)
            return carry

        lax.fori_loop(0, n // 2 - 1, body, 0)
        scores(0, kc_ref[...], cw)
        softmax(1, tk)
        pv(0, vl_ref[(n - 2) * tk:(n - 1) * tk, :], tk)
        softmax(0, cw)
        pv(1, vl_ref[(n - 1) * tk:n * tk, :], tk)
        pv(0, vc_ref[...], cw)
    else:
        scores(0, kc_ref[...], cw)
        softmax(0, cw)
        pv(0, vc_ref[...], cw)
    acc = acc_scr[...]
    o_ref[...] = (acc[:, 0:MLA_V] / acc[:, MLA_V:MLA_V + 1]).astype(o_ref.dtype)


def _attn_scratch(tq, tk):
    return [pltpu.VMEM((tq, tk), F32), pltpu.VMEM((tq, tk), F32), pltpu.VMEM((tq, tk), BF16),
            pltpu.VMEM((tq, tk), BF16), pltpu.VMEM((tq, 1), F32), pltpu.VMEM((tq, 1), F32),
            pltpu.VMEM((tq, V_EXT), F32), pltpu.VMEM((tq, 1), F32)]


def _attention_latent(q, k, v, geo):
    b, l, c, nl, nt = geo["B"], geo["L"], geo["C"], geo["NL"], geo["NT"]
    tq, tk = geo["TQ"], geo["TK"]
    nq = l // tq
    cblk = nl // c
    assert (l // tk) % 2 == 0
    return pl.pallas_call(
        functools.partial(_attn_kernel, with_latent=True, tk=tk, n_lat_tiles=l // tk),
        grid=(b, MLA_HEADS, nq),
        in_specs=[pl.BlockSpec((None, tq, QK_DIM), lambda bi, h, i: (h, bi * nq + i, 0)),
                  pl.BlockSpec((None, l, QK_DIM), lambda bi, h, i: (h, bi, 0)),
                  pl.BlockSpec((None, l, V_EXT), lambda bi, h, i: (h, bi, 0)),
                  pl.BlockSpec((None, c, QK_DIM), lambda bi, h, i: (h, cblk + bi, 0)),
                  pl.BlockSpec((None, c, V_EXT), lambda bi, h, i: (h, cblk + bi, 0))],
        out_specs=pl.BlockSpec((tq, MLA_V), lambda bi, h, i: (bi * nq + i, h)),
        out_shape=jax.ShapeDtypeStruct((nl, MLA_HEADS * MLA_V), BF16),
        scratch_shapes=_attn_scratch(tq, tk),
        compiler_params=_cparams(("arbitrary", "arbitrary", "arbitrary")),
        name="mla_attention",
    )(q, k, v, k, v)


def _attention_context(q, k, v, geo):
    b, c, nl, nc = geo["B"], geo["C"], geo["NL"], geo["NC"]
    cblk = nl // c
    cmap = lambda bi, h: (h, cblk + bi, 0)
    return pl.pallas_call(
        functools.partial(_attn_kernel, with_latent=False, tk=c, n_lat_tiles=0),
        grid=(b, MLA_HEADS),
        in_specs=[pl.BlockSpec((None, c, QK_DIM), cmap),
                  pl.BlockSpec((None, c, QK_DIM), cmap),
                  pl.BlockSpec((None, c, V_EXT), cmap)],
        out_specs=pl.BlockSpec((c, MLA_V), lambda bi, h: (bi, h)),
        out_shape=jax.ShapeDtypeStruct((nc, MLA_HEADS * MLA_V), BF16),
        scratch_shapes=_attn_scratch(c, c),
        compiler_params=_cparams(("arbitrary", "arbitrary")),
        name="mla_attention_ctx",
    )(q, k, v)


def _ret_kernel(lg_ref, lgq_ref, lgcol_ref, rql_ref, rkl_ref, rvl_ref, rgl_ref,
                rqc_ref, rkc_ref, rvc_ref, rgc_ref, *rest, l, c, ctx_out):
    if ctx_out:
        yl_ref, yc_ref, sf_scr, m_scr = rest
    else:
        yl_ref, sf_scr, m_scr = rest
        yc_ref = None
    cn = RET_CHUNK
    ncc = c // cn
    ncl = l // cn
    fi = lax.broadcasted_iota(jnp.int32, (cn, 1), 0).astype(F32)
    lgf_q = lgq_ref[0:1, :]
    lgb_q = lgq_ref[1:2, :]
    qdec_f = jnp.exp((fi + 1.0) * lgf_q)
    qdec_b = jnp.exp((float(cn) - fi) * lgb_q)
    kdec_f = jnp.exp((float(cn - 1) - fi) * lgf_q)
    kdec_b = jnp.exp(fi * lgb_q)
    sdec_f = jnp.exp(float(cn) * lgcol_ref[:, 0:1])
    sdec_b = jnp.exp(float(cn) * lgcol_ref[:, 1:2])

    ri = lax.broadcasted_iota(jnp.int32, (cn, cn), 0)
    ci = lax.broadcasted_iota(jnp.int32, (cn, cn), 1)
    diff = (ri - ci).astype(F32)
    for hd in range(RET_HEADS):
        mf = jnp.where(diff >= 0, jnp.exp(jnp.maximum(diff, 0.0) * lg_ref[0, hd]), 0.0)
        mb = jnp.where(diff <= 0, jnp.exp(jnp.maximum(-diff, 0.0) * lg_ref[1, hd]), 0.0)
        m_scr[hd] = mf + mb

    lane_q = lax.broadcasted_iota(jnp.int32, (1, RET_QK), 1) // RET_DK
    lane_v = lax.broadcasted_iota(jnp.int32, (1, RET_V), 1) // RET_DV
    bd = (lax.broadcasted_iota(jnp.int32, (RET_QK, RET_V), 0) // RET_DK
          == lax.broadcasted_iota(jnp.int32, (RET_QK, RET_V), 1) // RET_DV)

    def kv_outer(kf, v, kdec):
        kt = (kf * kdec).T.astype(BF16)
        return jnp.where(bd, _dot(kt, v), 0.0)

    def out_chunk(q, k, v, g, s_f, s_b):
        qf = q.astype(F32)
        ps = []
        for hd in range(RET_HEADS):
            qm = jnp.where(lane_q == hd, q, jnp.zeros_like(q))
            ps.append((_dot_nt(qm, k) * m_scr[hd]).astype(BF16))
        pcat = jnp.concatenate(ps, axis=1)
        vbd = jnp.concatenate([jnp.where(lane_v == hd, v, jnp.zeros_like(v)) for hd in range(RET_HEADS)], axis=0)
        o = _dot(pcat, vbd)
        o = o + _dot((qf * qdec_f).astype(BF16), s_f.astype(BF16))
        o = o + _dot((qf * qdec_b).astype(BF16), s_b.astype(BF16))
        o2 = o * o
        ms = jnp.zeros_like(o)
        for hd in range(RET_HEADS):
            sel = lane_v == hd
            hs = jnp.sum(jnp.where(sel, o2, 0.0), axis=-1, keepdims=True) * (1.0 / RET_DV)
            ms = jnp.where(sel, hs, ms)
        return (g * jax.nn.sigmoid(g)) * (o * lax.rsqrt(ms + RMS_EPS))

    zero_s = jnp.zeros((RET_QK, RET_V), F32)

    s = zero_s
    for cc in range(ncc):
        sf_scr[cc] = s
        sl = slice(cc * cn, (cc + 1) * cn)
        s = s * sdec_f + kv_outer(rkc_ref[sl, :].astype(F32), rvc_ref[sl, :], kdec_f)

    def fwd_body(n, s):
        sf_scr[ncc + n] = s
        off = pl.multiple_of(n * cn, cn)
        return s * sdec_f + kv_outer(rkl_ref[pl.ds(off, cn), :].astype(F32), rvl_ref[pl.ds(off, cn), :], kdec_f)

    lax.fori_loop(0, ncl, fwd_body, s)

    s = zero_s
    for cc in reversed(range(ncc)):
        sl = slice(cc * cn, (cc + 1) * cn)
        k = rkc_ref[sl, :]
        v = rvc_ref[sl, :]
        if ctx_out:
            yc_ref[sl, :] = out_chunk(rqc_ref[sl, :], k, v, rgc_ref[sl, :], sf_scr[cc], s).astype(yc_ref.dtype)
        s = s * sdec_b + kv_outer(k.astype(F32), v, kdec_b)

    def bwd_body(i, s):
        n = ncl - 1 - i
        off = pl.multiple_of(n * cn, cn)
        k = rkl_ref[pl.ds(off, cn), :]
        v = rvl_ref[pl.ds(off, cn), :]
        y = out_chunk(rql_ref[pl.ds(off, cn), :], k, v, rgl_ref[pl.ds(off, cn), :], sf_scr[ncc + n], s)
        yl_ref[pl.ds(off, cn), :] = y.astype(yl_ref.dtype)
        return s * sdec_b + kv_outer(k.astype(F32), v, kdec_b)

    lax.fori_loop(0, ncl, bwd_body, s)


def _retention(rq, rk, rv, rg, lw, geo, ctx_out):
    b, l, c, nl = geo["B"], geo["L"], geo["C"], geo["NL"]
    cblk = nl // c
    lat = lambda w: pl.BlockSpec((l, w), lambda bi: (bi, 0))
    ctx = lambda w: pl.BlockSpec((c, w), lambda bi: (cblk + bi, 0))
    out_specs = [pl.BlockSpec((l, RET_V), lambda bi: (bi, 0))]
    out_shape = [jax.ShapeDtypeStruct((nl, RET_V), BF16)]
    if ctx_out:
        out_specs.append(pl.BlockSpec((c, RET_V), lambda bi: (bi, 0)))
        out_shape.append(jax.ShapeDtypeStruct((b * c, RET_V), BF16))
    nchunks = (l + c) // RET_CHUNK
    return pl.pallas_call(
        functools.partial(_ret_kernel, l=l, c=c, ctx_out=ctx_out),
        grid=(b,),
        in_specs=[pl.BlockSpec(memory_space=pltpu.SMEM),
                  pl.BlockSpec((2, RET_QK), lambda bi: (0, 0)),
                  pl.BlockSpec((RET_QK, 2), lambda bi: (0, 0)),
                  lat(RET_QK), lat(RET_QK), lat(RET_V), lat(RET_V),
                  ctx(RET_QK), ctx(RET_QK), ctx(RET_V), ctx(RET_V)],
        out_specs=out_specs, out_shape=out_shape,
        scratch_shapes=[pltpu.VMEM((nchunks, RET_QK, RET_V), F32),
                        pltpu.VMEM((RET_HEADS, RET_CHUNK, RET_CHUNK), F32)],
        compiler_params=_cparams(("arbitrary",)),
        name="retention",
    )(lw["ret_lg"], lw["ret_lg_lane"], lw["ret_lg_col"], rq, rk, rv, rg, rq, rk, rv, rg)


def _outproj_kernel(yc_ref, oml_ref, omc_ref, yrl_ref, yrc_ref, x_ref, mod_ref, w0_ref, w1_ref, w2_ref, g2_ref,
                    rw_ref, rb_ref, xo_ref, h2_ref, idx_ref, gate_ref, *, nlt):
    d = D_MODEL
    is_lat = pl.program_id(0) < nlt
    om = jnp.where(is_lat, oml_ref[...], omc_ref[...])
    yr = jnp.where(is_lat, yrl_ref[...], yrc_ref[...])
    mix = _dot(yc_ref[...], w0_ref[...]) + _dot(om, w1_ref[...]) + _dot(yr, w2_ref[...])
    x = x_ref[...] + mod_ref[:, 2 * d:3 * d] * mix
    xo_ref[...] = x
    h2 = _rms(x) * g2_ref[...] * (1.0 + mod_ref[:, 4 * d:5 * d]) + mod_ref[:, 3 * d:4 * d]
    h2_ref[...] = h2
    lg = _dot(h2, rw_ref[...], precision=lax.Precision.HIGHEST) + rb_ref[...]
    lane = lax.broadcasted_iota(jnp.int32, lg.shape, 1)
    idx_out = jnp.zeros(lg.shape, jnp.int32)
    gate_out = jnp.zeros(lg.shape, F32)
    v0 = None
    den = None
    es = []
    for kk in range(TOP_K):
        m = jnp.max(lg, axis=-1, keepdims=True)
        idx = jnp.min(jnp.where(lg == m, lane, N_EXPERTS - 1), axis=-1, keepdims=True)
        lg = jnp.where(lane == idx, -jnp.inf, lg)
        if kk == 0:
            v0 = m
        e = jnp.exp(m - v0)
        es.append(e)
        den = e if den is None else den + e
        idx_out = jnp.where(lane == kk, idx, idx_out)
    for kk in range(TOP_K):
        gate_out = jnp.where(lane == kk, es[kk] / den, gate_out)
    idx_ref[...] = idx_out
    gate_ref[...] = gate_out


def _outproj(yc, om_l, om_c, yr_l, yr_c, x, mod3, lw, geo, n_rows):
    tm, nlt, l, b = geo["TM"], geo["NLT"], geo["L"], geo["B"]
    d = D_MODEL
    row = lambda t: (t, 0)
    latrow = lambda t: (jnp.minimum(t, nlt - 1), 0)
    ctxrow = lambda t: (jnp.maximum(t - nlt, 0), 0)
    const2 = lambda t: (0, 0)
    modrow = lambda t: (jnp.where(t < nlt, (t * tm) // l, b), 0, 0)
    return pl.pallas_call(
        functools.partial(_outproj_kernel, nlt=nlt),
        grid=(n_rows // tm,),
        in_specs=[pl.BlockSpec((tm, CONV_CH), row),
                  pl.BlockSpec((tm, MLA_HEADS * MLA_V), latrow),
                  pl.BlockSpec((tm, MLA_HEADS * MLA_V), ctxrow),
                  pl.BlockSpec((tm, RET_V), latrow),
                  pl.BlockSpec((tm, RET_V), ctxrow),
                  pl.BlockSpec((tm, d), row),
                  pl.BlockSpec((None, 1, 6 * d), modrow),
                  pl.BlockSpec((CONV_CH, d), const2),
                  pl.BlockSpec((MLA_HEADS * MLA_V, d), const2),
                  pl.BlockSpec((RET_V, d), const2),
                  pl.BlockSpec((1, d), const2),
                  pl.BlockSpec((d, LANE), const2),
                  pl.BlockSpec((1, LANE), const2)],
        out_specs=[pl.BlockSpec((tm, d), row), pl.BlockSpec((tm, d), row),
                   pl.BlockSpec((tm, LANE), row), pl.BlockSpec((tm, LANE), row)],
        out_shape=[jax.ShapeDtypeStruct((n_rows, d), F32), jax.ShapeDtypeStruct((n_rows, d), F32),
                   jax.ShapeDtypeStruct((n_rows, LANE), jnp.int32), jax.ShapeDtypeStruct((n_rows, LANE), F32)],
        compiler_params=_cparams(("arbitrary",)),
        name="outproj_router",
    )(yc, om_l, om_c, yr_l, yr_c, x, mod3, lw["w_out0"], lw["w_out1"], lw["w_out2"], lw["g2"], lw["router_w"], lw["router_b"])


def _moe_kernel(be_ref, nreal_ref, src_ref, srcn_ref, dst_ref, gate_ref, h2_hbm, w1_ref, b1_ref, w2_ref, b2_ref,
                y4_hbm, xbuf, act_scr, ybuf, sem_g, sem_s, *, bm):
    del be_ref
    i = pl.program_id(0)
    nreal = nreal_ref[0]
    slot = lax.rem(i, 2)

    def gather_copy(sl, r, tok):
        return pltpu.make_async_copy(h2_hbm.at[pl.ds(tok, 1), :], xbuf.at[sl, pl.ds(r, 1), :], sem_g.at[sl])

    def scatter_copy(r, row):
        return pltpu.make_async_copy(ybuf.at[pl.ds(r, 1), :], y4_hbm.at[pl.ds(row, 1), :], sem_s)

    @pl.when(i == 0)
    def _():
        for r in range(bm):
            gather_copy(0, r, src_ref[0, r]).start()

    @pl.when(i < nreal)
    def _():
        for r in range(bm):
            gather_copy(1 - slot, r, srcn_ref[0, r]).start()
        for r in range(bm):
            gather_copy(slot, r, 0).wait()
        hid = _dot(xbuf[slot].astype(BF16), w1_ref[...]) + b1_ref[...]
        glu = jnp.minimum(hid[:, :D_FF], SWIGLU_LIMIT)
        lin = jnp.clip(hid[:, D_FF:], -SWIGLU_LIMIT, SWIGLU_LIMIT)
        act_scr[...] = (glu * jax.nn.sigmoid(SWIGLU_ALPHA * glu) * (lin + 1.0)).astype(BF16)

    @pl.when(jnp.logical_and(i > 0, i <= nreal))
    def _():
        for r in range(bm):
            scatter_copy(r, 0).wait()

    @pl.when(i < nreal)
    def _():
        ybuf[...] = (_dot(act_scr[...], w2_ref[...]) + b2_ref[...]) * gate_ref[...]
        for r in range(bm):
            scatter_copy(r, dst_ref[0, r]).start()

    @pl.when(i == nreal)
    def _():
        for r in range(bm):
            gather_copy(slot, r, 0).wait()


def _moe(h2, top_idx, gates, lw, geo):
    n = h2.shape[0]
    d = D_MODEL
    bm = geo["BM"]
    a = n * TOP_K
    n_blocks = a // bm + N_EXPERTS
    n_slots = n_blocks * bm

    e_flat = top_idx.reshape(a)
    order = jnp.argsort(e_flat).astype(jnp.int32)
    counts = jnp.bincount(e_flat, length=N_EXPERTS).astype(jnp.int32)
    padded = (counts + bm - 1) // bm * bm
    padded_end = jnp.cumsum(padded)
    padded_start = padded_end - padded
    group_start = jnp.cumsum(counts) - counts
    blk = jnp.arange(n_blocks, dtype=jnp.int32)
    block_expert = jnp.minimum(jnp.searchsorted(padded_end, blk * bm, side="right"), N_EXPERTS - 1).astype(jnp.int32)
    n_real = (padded_end[-1] // bm).astype(jnp.int32).reshape(1)
    slot = jnp.arange(n_slots, dtype=jnp.int32)
    slot_e = jnp.repeat(block_expert, bm)
    rank = slot - padded_start[slot_e]
    valid = (rank < counts[slot_e]) & (slot < padded_end[-1])
    sp = jnp.clip(group_start[slot_e] + rank, 0, a - 1)
    assign = order[sp]
    tok = assign // TOP_K
    slot_src = jnp.where(valid, tok, 0).astype(jnp.int32)
    slot_dst = jnp.where(valid, (assign % TOP_K) * n + tok, a + slot % bm).astype(jnp.int32)
    slot_gate = jnp.where(valid, gates.reshape(a)[assign], 0.0).astype(F32)
    src3 = slot_src.reshape(n_blocks, 1, bm)

    smem_cur = pl.BlockSpec((None, 1, bm), lambda i, be, nr: (i, 0, 0), memory_space=pltpu.SMEM)
    smem_nxt = pl.BlockSpec((None, 1, bm), lambda i, be, nr: (jnp.minimum(i + 1, n_blocks - 1), 0, 0),
                            memory_space=pltpu.SMEM)
    return pl.pallas_call(
        functools.partial(_moe_kernel, bm=bm),
        grid_spec=pltpu.PrefetchScalarGridSpec(
            num_scalar_prefetch=2,
            grid=(n_blocks,),
            in_specs=[smem_cur, smem_nxt, smem_cur,
                      pl.BlockSpec((None, bm, 1), lambda i, be, nr: (i, 0, 0)),
                      pl.BlockSpec(memory_space=pl.ANY),
                      pl.BlockSpec((None, d, 2 * D_FF), lambda i, be, nr: (be[i], 0, 0)),
                      pl.BlockSpec((None, 1, 2 * D_FF), lambda i, be, nr: (be[i], 0, 0)),
                      pl.BlockSpec((None, D_FF, d), lambda i, be, nr: (be[i], 0, 0)),
                      pl.BlockSpec((None, 1, d), lambda i, be, nr: (be[i], 0, 0))],
            out_specs=pl.BlockSpec(memory_space=pl.ANY),
            scratch_shapes=[pltpu.VMEM((2, bm, d), F32), pltpu.VMEM((bm, D_FF), BF16), pltpu.VMEM((bm, d), F32),
                            pltpu.SemaphoreType.DMA((2,)), pltpu.SemaphoreType.DMA(())]),
        out_shape=jax.ShapeDtypeStruct((a + bm, d), F32),
        compiler_params=_cparams(("arbitrary",)),
        name="moe_experts",
    )(block_expert, n_real, src3, src3, slot_dst.reshape(n_blocks, 1, bm),
      slot_gate.reshape(n_blocks, bm, 1), h2, lw["moe_w1"], lw["moe_b1"], lw["moe_w2"], lw["moe_b2"])


def _final_kernel(x_ref, ya_ref, yb_ref, yc_ref, yd_ref, mod_ref, g_ref, o_ref):
    d = D_MODEL
    y = (ya_ref[...] + yb_ref[...]) + (yc_ref[...] + yd_ref[...])
    x = x_ref[...] + mod_ref[:, 5 * d:6 * d] * y
    o_ref[...] = _rms(x) * g_ref[...]


def _final(x, y4, mod3, g_final, geo):
    tm, l, nl = geo["TM"], geo["L"], geo["NL"]
    d = D_MODEL
    row = lambda t: (t, 0)
    return pl.pallas_call(
        _final_kernel,
        grid=(nl // tm,),
        in_specs=[pl.BlockSpec((tm, d), row)] + _y4_specs(tm, nl) + [
                  pl.BlockSpec((None, 1, 6 * d), lambda t: ((t * tm) // l, 0, 0)),
                  pl.BlockSpec((1, d), lambda t: (0, 0))],
        out_specs=pl.BlockSpec((tm, d), row),
        out_shape=jax.ShapeDtypeStruct((nl, d), F32),
        compiler_params=_cparams(("arbitrary",)),
        name="final_norm",
    )(x, y4, y4, y4, y4, mod3, g_final)


def _rot_cols(w, half):
    return jnp.concatenate([-w[:, half:2 * half], w[:, :half]], axis=1)


def _pack_layer(p, li):
    d = D_MODEL
    w_in = p["w_in"][li]
    o = 0
    a = w_in[:, 0:256]
    g = w_in[:, 256:512]
    cq = w_in[:, 512:896]
    rq = w_in[:, 896:1024]
    rg = w_in[:, 1024:1280]
    ckv = w_in[:, 1280:1536]
    kr = w_in[:, 1536:1600]
    rk = w_in[:, 1600:1728]
    rv = w_in[:, 1728:1984]
    del o

    def head_rot(w):
        return jnp.concatenate([_rot_cols(w[:, h * RET_DK:(h + 1) * RET_DK], RET_DK // 2)
                                for h in range(RET_HEADS)], axis=1)

    zpad = jnp.zeros((d, LANE - MLA_ROPE), F32)
    w_in_ext = jnp.concatenate(
        [a, g, cq, rq, head_rot(rq), rg, ckv, rk, head_rot(rk), rv,
         kr, zpad, _rot_cols(kr, MLA_ROPE // 2), zpad], axis=1).astype(BF16)

    w_uq = p["mla_w_uq"][li]
    hq = MLA_NOPE + MLA_ROPE
    zq = jnp.zeros((MLA_Q_RANK, LANE - MLA_ROPE), F32)
    wn = [w_uq[:, h * hq:h * hq + MLA_NOPE] for h in range(MLA_HEADS)]
    wr = [jnp.concatenate([w_uq[:, h * hq + MLA_NOPE:(h + 1) * hq], zq], axis=1) for h in range(MLA_HEADS)]
    wrt = [jnp.concatenate([_rot_cols(w_uq[:, h * hq + MLA_NOPE:(h + 1) * hq], MLA_ROPE // 2), zq], axis=1)
           for h in range(MLA_HEADS)]
    w_uq_ext = jnp.concatenate(wn + wr + wrt, axis=1).astype(BF16)

    w_ukv = p["mla_w_ukv"][li]
    hk = MLA_NOPE + MLA_V
    wk = [w_ukv[:, h * hk:h * hk + MLA_NOPE] for h in range(MLA_HEADS)]
    wv = [w_ukv[:, h * hk + MLA_NOPE:(h + 1) * hk] for h in range(MLA_HEADS)]
    w_ukv_ext = jnp.concatenate(wk + wv, axis=1).astype(BF16)

    lgf = jax.nn.log_sigmoid(p["ret_decay_fwd"][li].astype(F32))
    lgb = jax.nn.log_sigmoid(p["ret_decay_bwd"][li].astype(F32))
    lg = jnp.stack([lgf, lgb])
    lg_lane = jnp.repeat(lg, RET_DK, axis=1)

    w_out = p["w_out"][li].astype(BF16)
    rw = jnp.zeros((d, LANE), F32).at[:, :N_EXPERTS].set(p["router_w"][li])
    rb = jnp.full((1, LANE), NEG_BIG, F32).at[0, :N_EXPERTS].set(p["router_b"][li].astype(F32))
    return {
        "g1": p["g_norm1"][li].reshape(1, d), "g2": p["g_norm2"][li].reshape(1, d),
        "w_in": w_in_ext, "qn": p["mla_q_norm"][li].reshape(1, -1), "w_uq": w_uq_ext,
        "kvn": p["mla_kv_norm"][li].reshape(1, -1), "w_ukv": w_ukv_ext,
        "conv_w": p["conv_w"][li], "conv_b": p["conv_b"][li].reshape(1, -1),
        "conv_g": p["conv_ln_g"][li].reshape(1, -1), "conv_beta": p["conv_ln_b"][li].reshape(1, -1),
        "ret_lg": lg, "ret_lg_lane": lg_lane, "ret_lg_col": lg_lane.T,
        "w_out0": w_out[0:CONV_CH], "w_out1": w_out[CONV_CH:CONV_CH + MLA_HEADS * MLA_V],
        "w_out2": w_out[CONV_CH + MLA_HEADS * MLA_V:],
        "router_w": rw, "router_b": rb,
        "moe_w1": p["moe_w1"][li].astype(BF16), "moe_b1": p["moe_b1"][li].reshape(N_EXPERTS, 1, -1),
        "moe_w2": p["moe_w2"][li].astype(BF16), "moe_b2": p["moe_b2"][li].reshape(N_EXPERTS, 1, -1),
    }


def _position_tables(l, c, tm):
    rows = l // GRID_W
    row = jnp.repeat(jnp.arange(rows), GRID_W).astype(F32)
    col = jnp.tile(jnp.arange(GRID_W), rows).astype(F32)
    npa = MLA_ROPE // 4
    freq = ROPE_BASE ** (-jnp.arange(npa, dtype=F32) / npa)
    ang = jnp.concatenate([row[:, None] * freq, col[:, None] * freq], axis=-1)
    zl = jnp.zeros((l, LANE - MLA_ROPE), F32)
    mla_lat = jnp.concatenate([jnp.cos(ang), jnp.cos(ang), zl, jnp.sin(ang), jnp.sin(ang), zl], axis=1)
    mla_ctx = jnp.concatenate([jnp.ones((tm, LANE), F32), jnp.zeros((tm, LANE), F32)], axis=1)
    mla = jnp.concatenate([mla_lat, mla_ctx], axis=0)

    theta = 1.0 / (ROPE_BASE ** jnp.linspace(0.0, 1.0, RET_DK // 2, dtype=F32))

    def ret_tab(pos):
        a = pos.astype(F32)[:, None] * theta
        cs = jnp.tile(jnp.concatenate([jnp.cos(a), jnp.cos(a)], axis=1), (1, RET_HEADS))
        sn = jnp.tile(jnp.concatenate([jnp.sin(a), jnp.sin(a)], axis=1), (1, RET_HEADS))
        return jnp.concatenate([cs, sn], axis=1)

    ret = jnp.concatenate([ret_tab(c + jnp.arange(l)), jnp.tile(ret_tab(jnp.arange(c)), (tm // c, 1))], axis=0)
    return {"mla": mla, "ret": ret}


def kernel(x, c, ctx, c_ctx, w_ada, b_ada, g_norm1, g_norm2, w_in, w_out, conv_w, conv_b, conv_ln_g, conv_ln_b,
           mla_q_norm, mla_w_uq, mla_kv_norm, mla_w_ukv, ret_decay_fwd, ret_decay_bwd, router_w, router_b,
           moe_w1, moe_b1, moe_w2, moe_b2, g_final):
    b, l, d = x.shape
    cl = ctx.shape[1]
    depth = w_ada.shape[0]
    tm = 512
    nl, nc = b * l, b * cl
    nt = nl + nc
    assert d == D_MODEL and l % tm == 0 and nc % tm == 0 and tm % cl == 0 and cl % CONV_SUB == 0
    geo = {"B": b, "L": l, "C": cl, "NL": nl, "NC": nc, "NT": nt, "TM": tm, "NLT": nl // tm, "LT": l // tm,
           "TQ": min(1024, l), "TK": min(512, l), "BM": 512}
    p = {"w_in": w_in, "w_out": w_out, "g_norm1": g_norm1, "g_norm2": g_norm2, "conv_w": conv_w, "conv_b": conv_b,
         "conv_ln_g": conv_ln_g, "conv_ln_b": conv_ln_b, "mla_q_norm": mla_q_norm, "mla_w_uq": mla_w_uq,
         "mla_kv_norm": mla_kv_norm, "mla_w_ukv": mla_w_ukv, "ret_decay_fwd": ret_decay_fwd,
         "ret_decay_bwd": ret_decay_bwd, "router_w": router_w, "router_b": router_b, "moe_w1": moe_w1,
         "moe_b1": moe_b1, "moe_w2": moe_w2, "moe_b2": moe_b2}

    r = -(-(b + 1) // 8) * 8
    cs = jnp.zeros((r, d), F32).at[:b].set(c).at[b].set(c_ctx)
    mod = _modulation(cs, w_ada, b_ada)
    tabs = _position_tables(l, cl, tm)

    xa = jnp.concatenate([x.reshape(nl, d), ctx.reshape(nc, d)], axis=0)
    y4 = None
    for li in range(depth):
        lw = _pack_layer(p, li)
        mod3 = mod[li].reshape(r, 1, 6 * d)
        update_ctx = li < depth - 1
        if y4 is None:
            z, q, k, v, rq, rk, rv, rg = _inproj(xa, mod3, lw, tabs, geo)
        else:
            xa, z, q, k, v, rq, rk, rv, rg = _inproj(xa, mod3, lw, tabs, geo, y4=y4,
                                                     modp3=mod[li - 1].reshape(r, 1, 6 * d))
        n_rows = nt if update_ctx else nl
        yc = _conv(z, lw, geo, n_rows)
        om_l = _attention_latent(q, k, v, geo)
        if update_ctx:
            om_c = _attention_context(q, k, v, geo)
            yr_l, yr_c = _retention(rq, rk, rv, rg, lw, geo, True)
        else:
            (yr_l,) = _retention(rq, rk, rv, rg, lw, geo, False)
            om_c, yr_c = om_l, yr_l
        xa, h2, idx, gates = _outproj(yc, om_l, om_c, yr_l, yr_c, xa, mod3, lw, geo, n_rows)
        y4 = _moe(h2, idx[:, :TOP_K], gates[:, :TOP_K], lw, geo)
    out = _final(xa, y4, mod[depth - 1].reshape(r, 1, 6 * d), g_final.reshape(1, d), geo)
    return out.reshape(b, l, d)
```

```python
import functools

import jax
import jax.numpy as jnp
from jax import lax
from jax.experimental import pallas as pl
from jax.experimental.pallas import tpu as pltpu

F32 = jnp.float32
BF16 = jnp.bfloat16

D_MODEL = 1024
CONV_CH = 256
CONV_WIDTH = 31
MLA_HEADS = 4
MLA_NOPE = 128
MLA_ROPE = 64
MLA_V = 128
MLA_Q_RANK = 384
MLA_KV_RANK = 256
RET_HEADS = 4
RET_DK = 32
RET_DV = 64
N_EXPERTS = 32
TOP_K = 4
D_FF = 1024
SWIGLU_LIMIT = 7.0
SWIGLU_ALPHA = 1.702
GRID_W = 64
ROPE_BASE = 10000.0
RMS_EPS = 1e-6
LN_EPS = 1e-5

LANE = 128
QK_DIM = 2 * LANE
V_EXT = 2 * LANE
ATT_ROWS = 64
LOG2_E = 1.4426950408889634
RET_QK = RET_HEADS * RET_DK
RET_V = RET_HEADS * RET_DV
CONV_HALO = 16
CONV_SUB = 256
RET_CHUNK = 128
NEG_BIG = -1e30
IDX_ROWS = 8

_O_A, _O_G, _O_CQ, _O_RQ, _O_RQT, _O_RG, _O_CKV, _O_RK, _O_RKT, _O_RV, _O_KR, _O_KRT, IN_EXT = (
    0, 256, 512, 896, 1024, 1152, 1408, 1664, 1792, 1920, 2176, 2304, 2432)

VMEM_LIMIT = 56 * 1024 * 1024


def _cparams(sem):
    return pltpu.CompilerParams(dimension_semantics=sem, vmem_limit_bytes=VMEM_LIMIT)


def _rms(x, eps=RMS_EPS):
    return x * lax.rsqrt(jnp.mean(x * x, axis=-1, keepdims=True) + eps)


def _dot(a, b, **kw):
    return jnp.dot(a, b, preferred_element_type=F32, **kw)


def _dot_nt(a, b):
    return lax.dot_general(a, b, (((1,), (1,)), ((), ())), preferred_element_type=F32)


def _mod_kernel(cs_ref, w_ref, b_ref, o_ref):
    s = cs_ref[...]
    s = s * jax.nn.sigmoid(s)
    o_ref[...] = _dot(s, w_ref[...], precision=lax.Precision.HIGHEST) + b_ref[...]


def _modulation(cs, w_ada, b_ada):
    depth, d, n = w_ada.shape
    r = cs.shape[0]
    tn = 1536
    return pl.pallas_call(
        _mod_kernel,
        grid=(depth, n // tn),
        in_specs=[pl.BlockSpec((r, d), lambda l, j: (0, 0)),
                  pl.BlockSpec((None, d, tn), lambda l, j: (l, 0, j)),
                  pl.BlockSpec((None, 1, tn), lambda l, j: (l, 0, j))],
        out_specs=pl.BlockSpec((None, r, tn), lambda l, j: (l, 0, j)),
        out_shape=jax.ShapeDtypeStruct((depth, r, n), F32),
        compiler_params=_cparams(("arbitrary", "arbitrary")),
        name="modulation",
    )(cs, w_ada, b_ada.reshape(depth, 1, n))


def _gated_sum(gate_ref, y_refs):
    g = gate_ref[...]
    terms = [g[:, kk:kk + 1] * y_refs[kk][...] for kk in range(TOP_K)]
    return (terms[0] + terms[1]) + (terms[2] + terms[3])


def _inproj_kernel(*refs, combine, nlt):
    if combine:
        (x_ref, ya_ref, yb_ref, yc_ref, yd_ref, gt_ref, modp_ref, mod_ref, g1_ref, win_ref, qn_ref, wuq_ref,
         kvn_ref, wukv_ref, mcs_ref, rcs_ref, xo_ref, z_ref, q_ref, k_ref, v_ref, rq_ref, rk_ref, rv_ref,
         rg_ref) = refs
    else:
        (xl_ref, xc_ref, mod_ref, g1_ref, win_ref, qn_ref, wuq_ref, kvn_ref, wukv_ref,
         mcs_ref, rcs_ref, z_ref, q_ref, k_ref, v_ref, rq_ref, rk_ref, rv_ref, rg_ref) = refs
    d = D_MODEL
    if combine:
        y = _gated_sum(gt_ref, (ya_ref, yb_ref, yc_ref, yd_ref))
        x = x_ref[...] + modp_ref[:, 5 * d:6 * d] * y
        xo_ref[...] = x
    else:
        x = jnp.where(pl.program_id(0) < nlt, xl_ref[...], xc_ref[...])
    h = _rms(x) * g1_ref[...] * (1.0 + mod_ref[:, d:2 * d]) + mod_ref[:, 0:d]
    u = _dot(h.astype(BF16), win_ref[...])

    z_ref[...] = u[:, _O_A:_O_G] * jax.nn.sigmoid(u[:, _O_G:_O_CQ])

    mc = mcs_ref[:, 0:LANE]
    ms = mcs_ref[:, LANE:2 * LANE]
    rc = rcs_ref[:, 0:LANE]
    rs = rcs_ref[:, LANE:2 * LANE]

    cqn = _rms(u[:, _O_CQ:_O_RQ]) * qn_ref[...]
    uq = _dot(cqn.astype(BF16), wuq_ref[...])
    scale = float(MLA_NOPE + MLA_ROPE) ** -0.5 * LOG2_E
    hw = MLA_HEADS * LANE
    for hd in range(MLA_HEADS):
        lo = hd * LANE
        q_ref[hd, :, 0:LANE] = (uq[:, lo:lo + LANE] * scale).astype(BF16)
        rope = uq[:, hw + lo:hw + lo + LANE] * mc + uq[:, 2 * hw + lo:2 * hw + lo + LANE] * ms
        q_ref[hd, :, LANE:2 * LANE] = (rope * scale).astype(BF16)

    ckvn = _rms(u[:, _O_CKV:_O_RK]) * kvn_ref[...]
    ukv = _dot(ckvn.astype(BF16), wukv_ref[...])
    krope = (u[:, _O_KR:_O_KRT] * mc + u[:, _O_KRT:IN_EXT] * ms).astype(BF16)
    ones_col = (lax.broadcasted_iota(jnp.int32, krope.shape, 1) == 0).astype(BF16)
    for hd in range(MLA_HEADS):
        lo = hd * LANE
        k_ref[hd, :, 0:LANE] = ukv[:, lo:lo + LANE].astype(BF16)
        k_ref[hd, :, LANE:2 * LANE] = krope
        v_ref[hd, :, 0:LANE] = ukv[:, hw + lo:hw + lo + LANE].astype(BF16)
        v_ref[hd, :, LANE:2 * LANE] = ones_col

    rq_ref[...] = (u[:, _O_RQ:_O_RQT] * rc + u[:, _O_RQT:_O_RG] * rs).astype(BF16)
    rk_ref[...] = ((u[:, _O_RK:_O_RKT] * rc + u[:, _O_RKT:_O_RV] * rs) * (float(RET_DK) ** -0.5)).astype(BF16)
    rv_ref[...] = u[:, _O_RV:_O_KR].astype(BF16)
    rg_ref[...] = u[:, _O_RG:_O_CKV]


def _y4_specs(tm, n):
    return [pl.BlockSpec((tm, D_MODEL), functools.partial(lambda t, kk: (kk * (n // tm) + t, 0), kk=kk))
            for kk in range(TOP_K)]


def _inproj(x, mod3, lw, tabs, geo, y4=None, gates=None, modp3=None):
    nt, tm, nlt, lt = geo["NT"], geo["TM"], geo["NLT"], geo["LT"]
    b, l = geo["B"], geo["L"]
    d = D_MODEL
    combine = y4 is not None
    row = lambda t: (t, 0)
    const2 = lambda t: (0, 0)
    modrow = lambda t: (jnp.where(t < nlt, (t * tm) // l, b), 0, 0)
    tabrow = lambda t: (jnp.where(t < nlt, t % lt, lt), 0)
    if combine:
        in_specs = [pl.BlockSpec((tm, d), row)] + _y4_specs(tm, nt) + [
            pl.BlockSpec((tm, LANE), row), pl.BlockSpec((None, 1, 6 * d), modrow)]
        args = [x] + [y4] * TOP_K + [gates, modp3]
    else:
        in_specs = [pl.BlockSpec((tm, d), lambda t: (jnp.minimum(t, nlt - 1), 0)),
                    pl.BlockSpec((tm, d), lambda t: (jnp.maximum(t - nlt, 0), 0))]
        args = [x[0], x[1]]
    in_specs += [
        pl.BlockSpec((None, 1, 6 * d), modrow),
        pl.BlockSpec((1, d), const2),
        pl.BlockSpec((d, IN_EXT), const2),
        pl.BlockSpec((1, MLA_Q_RANK), const2),
        pl.BlockSpec((MLA_Q_RANK, 3 * MLA_HEADS * LANE), const2),
        pl.BlockSpec((1, MLA_KV_RANK), const2),
        pl.BlockSpec((MLA_KV_RANK, 2 * MLA_HEADS * LANE), const2),
        pl.BlockSpec((tm, 2 * LANE), tabrow),
        pl.BlockSpec((tm, 2 * LANE), tabrow),
    ]
    args += [mod3, lw["g1"], lw["w_in"], lw["qn"], lw["w_uq"], lw["kvn"], lw["w_ukv"],
             tabs["mla"], tabs["ret"]]
    out_specs, out_shape = [], []
    if combine:
        out_specs.append(pl.BlockSpec((tm, d), row))
        out_shape.append(jax.ShapeDtypeStruct((nt, d), F32))
    hrow = lambda t: (0, t, 0)
    out_specs += [
        pl.BlockSpec((tm, CONV_CH), row),
        pl.BlockSpec((MLA_HEADS, tm, QK_DIM), hrow),
        pl.BlockSpec((MLA_HEADS, tm, QK_DIM), hrow),
        pl.BlockSpec((MLA_HEADS, tm, V_EXT), hrow),
        pl.BlockSpec((tm, RET_QK), row),
        pl.BlockSpec((tm, RET_QK), row),
        pl.BlockSpec((tm, RET_V), row),
        pl.BlockSpec((tm, RET_V), row),
    ]
    out_shape += [
        jax.ShapeDtypeStruct((nt, CONV_CH), F32),
        jax.ShapeDtypeStruct((MLA_HEADS, nt, QK_DIM), BF16),
        jax.ShapeDtypeStruct((MLA_HEADS, nt, QK_DIM), BF16),
        jax.ShapeDtypeStruct((MLA_HEADS, nt, V_EXT), BF16),
        jax.ShapeDtypeStruct((nt, RET_QK), BF16),
        jax.ShapeDtypeStruct((nt, RET_QK), BF16),
        jax.ShapeDtypeStruct((nt, RET_V), BF16),
        jax.ShapeDtypeStruct((nt, RET_V), F32),
    ]
    return pl.pallas_call(
        functools.partial(_inproj_kernel, combine=combine, nlt=nlt),
        grid=(nt // tm,),
        in_specs=in_specs, out_specs=out_specs, out_shape=out_shape,
        compiler_params=_cparams(("arbitrary",)),
        name="inproj",
    )(*args)


def _conv_kernel(z_ref, zp_ref, zn_ref, w_ref, b_ref, g_ref, beta_ref, o_ref, e_scr, *, tm, l, c, nl):
    t = pl.program_id(0)
    nsub = tm // CONV_SUB
    rows = 64
    for s in range(nsub):
        g0 = t * tm + s * CONV_SUB
        is_lat = g0 < nl
        pos0 = jnp.where(is_lat, lax.rem(g0, l), lax.rem(g0 - nl, c))
        seqlen = jnp.where(is_lat, l, c)
        first = pos0 == 0
        last = pos0 + CONV_SUB == seqlen
        lo = s * CONV_SUB
        prev = z_ref[lo - CONV_HALO:lo, :] if s > 0 else zp_ref[...]
        nxt = z_ref[lo + CONV_SUB:lo + CONV_SUB + CONV_HALO, :] if s < nsub - 1 else zn_ref[...]
        e_scr[0:CONV_HALO, :] = jnp.where(first, 0.0, prev)
        e_scr[CONV_HALO:CONV_HALO + CONV_SUB, :] = z_ref[lo:lo + CONV_SUB, :]
        e_scr[CONV_HALO + CONV_SUB:2 * CONV_HALO + CONV_SUB, :] = jnp.where(last, 0.0, nxt)
        for r0 in range(0, CONV_SUB, rows):
            acc = jnp.zeros((rows, CONV_CH), F32) + b_ref[...]
            for j in range(CONV_WIDTH):
                off = r0 + j + CONV_HALO - CONV_WIDTH // 2
                acc = acc + w_ref[j:j + 1, :] * e_scr[off:off + rows, :]
            mu = jnp.mean(acc, axis=-1, keepdims=True)
            cen = acc - mu
            var = jnp.mean(cen * cen, axis=-1, keepdims=True)
            y = cen * lax.rsqrt(var + LN_EPS) * g_ref[...] + beta_ref[...]
            o_ref[lo + r0:lo + r0 + rows, :] = (y * jax.nn.sigmoid(y)).astype(o_ref.dtype)


def _conv(z, lw, geo, n_rows):
    tm, l, c, nl, nt = geo["TM"], geo["L"], geo["C"], geo["NL"], geo["NT"]
    hb = tm // CONV_HALO
    nhb = nt // CONV_HALO
    return pl.pallas_call(
        functools.partial(_conv_kernel, tm=tm, l=l, c=c, nl=nl),
        grid=(n_rows // tm,),
        in_specs=[pl.BlockSpec((tm, CONV_CH), lambda t: (t, 0)),
                  pl.BlockSpec((CONV_HALO, CONV_CH), lambda t: (jnp.maximum(t * hb - 1, 0), 0)),
                  pl.BlockSpec((CONV_HALO, CONV_CH), lambda t: (jnp.minimum((t + 1) * hb, nhb - 1), 0)),
                  pl.BlockSpec((CONV_WIDTH, CONV_CH), lambda t: (0, 0)),
                  pl.BlockSpec((1, CONV_CH), lambda t: (0, 0)),
                  pl.BlockSpec((1, CONV_CH), lambda t: (0, 0)),
                  pl.BlockSpec((1, CONV_CH), lambda t: (0, 0))],
        out_specs=pl.BlockSpec((tm, CONV_CH), lambda t: (t, 0)),
        out_shape=jax.ShapeDtypeStruct((n_rows, CONV_CH), BF16),
        scratch_shapes=[pltpu.VMEM((CONV_SUB + 2 * CONV_HALO, CONV_CH), F32)],
        compiler_params=_cparams(("arbitrary",)),
        name="conv_module",
    )(z, z, z, lw["conv_w"], lw["conv_b"], lw["conv_g"], lw["conv_beta"])


def _attn_kernel(*refs, with_latent, tk, n_lat_tiles):
    if with_latent:
        q_ref, kl_ref, vl_ref, kc_ref, vc_ref, o_ref, s0, s1, p0, p1, a0, a1, acc_scr, m_scr = refs
    else:
        q_ref, kc_ref, vc_ref, o_ref, s0, s1, p0, p1, a0, a1, acc_scr, m_scr = refs
    s_b, p_b, a_b = (s0, s1), (p0, p1), (a0, a1)
    tq = q_ref.shape[0]
    cw = kc_ref.shape[0]
    m_scr[...] = jnp.full(m_scr.shape, -jnp.inf, F32)
    acc_scr[...] = jnp.zeros(acc_scr.shape, F32)

    def scores(slot, k, width):
        s_b[slot][:, 0:width] = _dot_nt(q_ref[...], k)

    def softmax(slot, width):
        for r0 in range(0, tq, ATT_ROWS):
            rows = slice(r0, r0 + ATT_ROWS)
            s = s_b[slot][rows, 0:width]
            m_old = m_scr[rows, :]
            m_new = jnp.maximum(m_old, jnp.max(s, axis=-1, keepdims=True))
            a_b[slot][rows, :] = jnp.exp2(m_old - m_new)
            m_scr[rows, :] = m_new
            p_b[slot][rows, 0:width] = jnp.exp2(s - m_new).astype(BF16)

    def pv(slot, v, width):
        acc_scr[...] = acc_scr[...] * a_b[slot][...] + _dot(p_b[slot][:, 0:width], v)

    def k_tile(t):
        return kl_ref[pl.ds(pl.multiple_of(t * tk, tk), tk), :]

    def v_tile(t):
        return vl_ref[pl.ds(pl.multiple_of(t * tk, tk), tk), :]

    if with_latent:
        n = n_lat_tiles
        scores(0, kl_ref[0:tk, :], tk)
        scores(1, kl_ref[tk:2 * tk, :], tk)
        softmax(0, tk)

        def body(jj, carry):
            t = 2 * jj
            scores(0, k_tile(t + 2), tk)
            softmax(1, tk)
            pv(0, v_tile(t), tk)
            scores(1, k_tile(t + 3), tk)
            softmax(0, tk)
            pv(1, v_tile(t + 1), tk)
            return carry

        lax.fori_loop(0, n // 2 - 1, body, 0)
        scores(0, kc_ref[...], cw)
        softmax(1, tk)
        pv(0, vl_ref[(n - 2) * tk:(n - 1) * tk, :], tk)
        softmax(0, cw)
        pv(1, vl_ref[(n - 1) * tk:n * tk, :], tk)
        pv(0, vc_ref[...], cw)
    else:
        scores(0, kc_ref[...], cw)
        softmax(0, cw)
        pv(0, vc_ref[...], cw)
    acc = acc_scr[...]
    o_ref[...] = (acc[:, 0:MLA_V] / acc[:, MLA_V:MLA_V + 1]).astype(o_ref.dtype)


def _attn_scratch(tq, tk):
    return [pltpu.VMEM((tq, tk), F32), pltpu.VMEM((tq, tk), F32), pltpu.VMEM((tq, tk), BF16),
            pltpu.VMEM((tq, tk), BF16), pltpu.VMEM((tq, 1), F32), pltpu.VMEM((tq, 1), F32),
            pltpu.VMEM((tq, V_EXT), F32), pltpu.VMEM((tq, 1), F32)]


def _attention_latent(q, k, v, geo):
    b, l, c, nl, nt = geo["B"], geo["L"], geo["C"], geo["NL"], geo["NT"]
    tq, tk = geo["TQ"], geo["TK"]
    nq = l // tq
    cblk = nl // c
    assert (l // tk) % 2 == 0
    return pl.pallas_call(
        functools.partial(_attn_kernel, with_latent=True, tk=tk, n_lat_tiles=l // tk),
        grid=(b, MLA_HEADS, nq),
        in_specs=[pl.BlockSpec((None, tq, QK_DIM), lambda bi, h, i: (h, bi * nq + i, 0)),
                  pl.BlockSpec((None, l, QK_DIM), lambda bi, h, i: (h, bi, 0)),
                  pl.BlockSpec((None, l, V_EXT), lambda bi, h, i: (h, bi, 0)),
                  pl.BlockSpec((None, c, QK_DIM), lambda bi, h, i: (h, cblk + bi, 0)),
                  pl.BlockSpec((None, c, V_EXT), lambda bi, h, i: (h, cblk + bi, 0))],
        out_specs=pl.BlockSpec((tq, MLA_V), lambda bi, h, i: (bi * nq + i, h)),
        out_shape=jax.ShapeDtypeStruct((nl, MLA_HEADS * MLA_V), BF16),
        scratch_shapes=_attn_scratch(tq, tk),
        compiler_params=_cparams(("arbitrary", "arbitrary", "arbitrary")),
        name="mla_attention",
    )(q, k, v, k, v)


def _attention_context(q, k, v, geo):
    b, c, nl, nc = geo["B"], geo["C"], geo["NL"], geo["NC"]
    cblk = nl // c
    cmap = lambda bi, h: (h, cblk + bi, 0)
    return pl.pallas_call(
        functools.partial(_attn_kernel, with_latent=False, tk=c, n_lat_tiles=0),
        grid=(b, MLA_HEADS),
        in_specs=[pl.BlockSpec((None, c, QK_DIM), cmap),
                  pl.BlockSpec((None, c, QK_DIM), cmap),
                  pl.BlockSpec((None, c, V_EXT), cmap)],
        out_specs=pl.BlockSpec((c, MLA_V), lambda bi, h: (bi, h)),
        out_shape=jax.ShapeDtypeStruct((nc, MLA_HEADS * MLA_V), BF16),
        scratch_shapes=_attn_scratch(c, c),
        compiler_params=_cparams(("arbitrary", "arbitrary")),
        name="mla_attention_ctx",
    )(q, k, v)


def _ret_kernel(lg_ref, lgq_ref, lgcol_ref, rql_ref, rkl_ref, rvl_ref, rgl_ref,
                rqc_ref, rkc_ref, rvc_ref, rgc_ref, *rest, l, c, ctx_out):
    if ctx_out:
        yl_ref, yc_ref, sf_scr, m_scr = rest
    else:
        yl_ref, sf_scr, m_scr = rest
        yc_ref = None
    cn = RET_CHUNK
    ncc = c // cn
    ncl = l // cn
    fi = lax.broadcasted_iota(jnp.int32, (cn, 1), 0).astype(F32)
    lgf_q = lgq_ref[0:1, :]
    lgb_q = lgq_ref[1:2, :]
    qdec_f = jnp.exp((fi + 1.0) * lgf_q)
    qdec_b = jnp.exp((float(cn) - fi) * lgb_q)
    kdec_f = jnp.exp((float(cn - 1) - fi) * lgf_q)
    kdec_b = jnp.exp(fi * lgb_q)
    sdec_f = jnp.exp(float(cn) * lgcol_ref[:, 0:1])
    sdec_b = jnp.exp(float(cn) * lgcol_ref[:, 1:2])

    ri = lax.broadcasted_iota(jnp.int32, (cn, cn), 0)
    ci = lax.broadcasted_iota(jnp.int32, (cn, cn), 1)
    diff = (ri - ci).astype(F32)
    for hd in range(RET_HEADS):
        mf = jnp.where(diff >= 0, jnp.exp(jnp.maximum(diff, 0.0) * lg_ref[0, hd]), 0.0)
        mb = jnp.where(diff <= 0, jnp.exp(jnp.maximum(-diff, 0.0) * lg_ref[1, hd]), 0.0)
        m_scr[hd] = mf + mb

    lane_q = lax.broadcasted_iota(jnp.int32, (1, RET_QK), 1) // RET_DK
    lane_v = lax.broadcasted_iota(jnp.int32, (1, RET_V), 1) // RET_DV
    bd = (lax.broadcasted_iota(jnp.int32, (RET_QK, RET_V), 0) // RET_DK
          == lax.broadcasted_iota(jnp.int32, (RET_QK, RET_V), 1) // RET_DV)

    def kv_outer(kf, v, kdec):
        kt = (kf * kdec).T.astype(BF16)
        return jnp.where(bd, _dot(kt, v), 0.0)

    def out_chunk(q, k, v, g, s_f, s_b):
        qf = q.astype(F32)
        ps = []
        for hd in range(RET_HEADS):
            qm = jnp.where(lane_q == hd, q, jnp.zeros_like(q))
            ps.append((_dot_nt(qm, k) * m_scr[hd]).astype(BF16))
        pcat = jnp.concatenate(ps, axis=1)
        vbd = jnp.concatenate([jnp.where(lane_v == hd, v, jnp.zeros_like(v)) for hd in range(RET_HEADS)], axis=0)
        o = _dot(pcat, vbd)
        o = o + _dot((qf * qdec_f).astype(BF16), s_f.astype(BF16))
        o = o + _dot((qf * qdec_b).astype(BF16), s_b.astype(BF16))
        o2 = o * o
        ms = jnp.zeros_like(o)
        for hd in range(RET_HEADS):
            sel = lane_v == hd
            hs = jnp.sum(jnp.where(sel, o2, 0.0), axis=-1, keepdims=True) * (1.0 / RET_DV)
            ms = jnp.where(sel, hs, ms)
        return (g * jax.nn.sigmoid(g)) * (o * lax.rsqrt(ms + RMS_EPS))

    zero_s = jnp.zeros((RET_QK, RET_V), F32)

    s = zero_s
    for cc in range(ncc):
        sf_scr[cc] = s
        sl = slice(cc * cn, (cc + 1) * cn)
        s = s * sdec_f + kv_outer(rkc_ref[sl, :].astype(F32), rvc_ref[sl, :], kdec_f)

    def fwd_body(n, s):
        sf_scr[ncc + n] = s
        off = pl.multiple_of(n * cn, cn)
        return s * sdec_f + kv_outer(rkl_ref[pl.ds(off, cn), :].astype(F32), rvl_ref[pl.ds(off, cn), :], kdec_f)

    lax.fori_loop(0, ncl, fwd_body, s)

    s = zero_s
    for cc in reversed(range(ncc)):
        sl = slice(cc * cn, (cc + 1) * cn)
        k = rkc_ref[sl, :]
        v = rvc_ref[sl, :]
        if ctx_out:
            yc_ref[sl, :] = out_chunk(rqc_ref[sl, :], k, v, rgc_ref[sl, :], sf_scr[cc], s).astype(yc_ref.dtype)
        s = s * sdec_b + kv_outer(k.astype(F32), v, kdec_b)

    def bwd_body(i, s):
        n = ncl - 1 - i
        off = pl.multiple_of(n * cn, cn)
        k = rkl_ref[pl.ds(off, cn), :]
        v = rvl_ref[pl.ds(off, cn), :]
        y = out_chunk(rql_ref[pl.ds(off, cn), :], k, v, rgl_ref[pl.ds(off, cn), :], sf_scr[ncc + n], s)
        yl_ref[pl.ds(off, cn), :] = y.astype(yl_ref.dtype)
        return s * sdec_b + kv_outer(k.astype(F32), v, kdec_b)

    lax.fori_loop(0, ncl, bwd_body, s)


def _retention(rq, rk, rv, rg, lw, geo, ctx_out):
    b, l, c, nl = geo["B"], geo["L"], geo["C"], geo["NL"]
    cblk = nl // c
    lat = lambda w: pl.BlockSpec((l, w), lambda bi: (bi, 0))
    ctx = lambda w: pl.BlockSpec((c, w), lambda bi: (cblk + bi, 0))
    out_specs = [pl.BlockSpec((l, RET_V), lambda bi: (bi, 0))]
    out_shape = [jax.ShapeDtypeStruct((nl, RET_V), BF16)]
    if ctx_out:
        out_specs.append(pl.BlockSpec((c, RET_V), lambda bi: (bi, 0)))
        out_shape.append(jax.ShapeDtypeStruct((b * c, RET_V), BF16))
    nchunks = (l + c) // RET_CHUNK
    return pl.pallas_call(
        functools.partial(_ret_kernel, l=l, c=c, ctx_out=ctx_out),
        grid=(b,),
        in_specs=[pl.BlockSpec(memory_space=pltpu.SMEM),
                  pl.BlockSpec((2, RET_QK), lambda bi: (0, 0)),
                  pl.BlockSpec((RET_QK, 2), lambda bi: (0, 0)),
                  lat(RET_QK), lat(RET_QK), lat(RET_V), lat(RET_V),
                  ctx(RET_QK), ctx(RET_QK), ctx(RET_V), ctx(RET_V)],
        out_specs=out_specs, out_shape=out_shape,
        scratch_shapes=[pltpu.VMEM((nchunks, RET_QK, RET_V), F32),
                        pltpu.VMEM((RET_HEADS, RET_CHUNK, RET_CHUNK), F32)],
        compiler_params=_cparams(("arbitrary",)),
        name="retention",
    )(lw["ret_lg"], lw["ret_lg_lane"], lw["ret_lg_col"], rq, rk, rv, rg, rq, rk, rv, rg)


def _outproj_kernel(yc_ref, oml_ref, omc_ref, yrl_ref, yrc_ref, xl_ref, xc_ref, mod_ref, w0_ref, w1_ref, w2_ref,
                    g2_ref, rwh_ref, rwl_ref, rb_ref, xo_ref, h2_ref, idx_ref, gate_ref, *, nlt):
    d = D_MODEL
    is_lat = pl.program_id(0) < nlt
    om = jnp.where(is_lat, oml_ref[...], omc_ref[...])
    yr = jnp.where(is_lat, yrl_ref[...], yrc_ref[...])
    mix = _dot(yc_ref[...], w0_ref[...]) + _dot(om, w1_ref[...]) + _dot(yr, w2_ref[...])
    x = jnp.where(is_lat, xl_ref[...], xc_ref[...]) + mod_ref[:, 2 * d:3 * d] * mix
    xo_ref[...] = x
    h2 = _rms(x) * g2_ref[...] * (1.0 + mod_ref[:, 4 * d:5 * d]) + mod_ref[:, 3 * d:4 * d]
    h2_ref[...] = h2
    h_hi = h2.astype(BF16)
    h_lo = (h2 - h_hi.astype(F32)).astype(BF16)
    lg = (_dot(h_hi, rwh_ref[...]) + (_dot(h_hi, rwl_ref[...]) + _dot(h_lo, rwh_ref[...]))) + rb_ref[...]
    lane = lax.broadcasted_iota(jnp.int32, lg.shape, 1)
    idx_out = jnp.zeros(lg.shape, jnp.int32)
    gate_out = jnp.zeros(lg.shape, F32)
    v0 = None
    den = None
    es = []
    for kk in range(TOP_K):
        m = jnp.max(lg, axis=-1, keepdims=True)
        idx = jnp.min(jnp.where(lg == m, lane, N_EXPERTS - 1), axis=-1, keepdims=True)
        lg = jnp.where(lane == idx, -jnp.inf, lg)
        if kk == 0:
            v0 = m
        e = jnp.exp(m - v0)
        es.append(e)
        den = e if den is None else den + e
        idx_out = jnp.where(lane == kk, idx, idx_out)
    for kk in range(TOP_K):
        gate_out = jnp.where(lane == kk, es[kk] / den, gate_out)
    idx_ref[...] = idx_out.T[0:IDX_ROWS, :]
    gate_ref[...] = gate_out


def _outproj(yc, om_l, om_c, yr_l, yr_c, x_l, x_c, mod3, lw, geo, n_rows):
    tm, nlt, l, b = geo["TM"], geo["NLT"], geo["L"], geo["B"]
    d = D_MODEL
    row = lambda t: (t, 0)
    latrow = lambda t: (jnp.minimum(t, nlt - 1), 0)
    ctxrow = lambda t: (jnp.maximum(t - nlt, 0), 0)
    const2 = lambda t: (0, 0)
    modrow = lambda t: (jnp.where(t < nlt, (t * tm) // l, b), 0, 0)
    return pl.pallas_call(
        functools.partial(_outproj_kernel, nlt=nlt),
        grid=(n_rows // tm,),
        in_specs=[pl.BlockSpec((tm, CONV_CH), row),
                  pl.BlockSpec((tm, MLA_HEADS * MLA_V), latrow),
                  pl.BlockSpec((tm, MLA_HEADS * MLA_V), ctxrow),
                  pl.BlockSpec((tm, RET_V), latrow),
                  pl.BlockSpec((tm, RET_V), ctxrow),
                  pl.BlockSpec((tm, d), latrow),
                  pl.BlockSpec((tm, d), ctxrow),
                  pl.BlockSpec((None, 1, 6 * d), modrow),
                  pl.BlockSpec((CONV_CH, d), const2),
                  pl.BlockSpec((MLA_HEADS * MLA_V, d), const2),
                  pl.BlockSpec((RET_V, d), const2),
                  pl.BlockSpec((1, d), const2),
                  pl.BlockSpec((d, LANE), const2),
                  pl.BlockSpec((d, LANE), const2),
                  pl.BlockSpec((1, LANE), const2)],
        out_specs=[pl.BlockSpec((tm, d), row), pl.BlockSpec((tm, d), row),
                   pl.BlockSpec((IDX_ROWS, tm), lambda t: (0, t)), pl.BlockSpec((tm, LANE), row)],
        out_shape=[jax.ShapeDtypeStruct((n_rows, d), F32), jax.ShapeDtypeStruct((n_rows, d), F32),
                   jax.ShapeDtypeStruct((IDX_ROWS, n_rows), jnp.int32), jax.ShapeDtypeStruct((n_rows, LANE), F32)],
        compiler_params=_cparams(("arbitrary",)),
        name="outproj_router",
    )(yc, om_l, om_c, yr_l, yr_c, x_l, x_c, mod3, lw["w_out0"], lw["w_out1"], lw["w_out2"], lw["g2"], lw["router_w_hi"], lw["router_w_lo"], lw["router_b"])


def _moe_kernel(be_ref, nreal_ref, src_ref, srcn_ref, dstp_ref, h2_hbm, w1_ref, b1_ref, w2_ref, b2_ref,
                y4_hbm, xb0, xb1, yb0, yb1, w1b, w2b, act_scr, sem_g, sem_s, *, bm):
    i = pl.program_id(0)
    nreal = nreal_ref[0]
    par = lax.rem(i, 2)
    xbs, ybs = (xb0, xb1), (yb0, yb1)

    def gather_copy(sl, r, tok):
        return pltpu.make_async_copy(h2_hbm.at[pl.ds(tok, 1), :], xbs[sl].at[pl.ds(r, 1), :], sem_g.at[sl])

    def scatter_copy(sl, r, row):
        return pltpu.make_async_copy(ybs[sl].at[pl.ds(r, 1), :], y4_hbm.at[pl.ds(row, 1), :], sem_s.at[sl])

    @pl.when(i == 0)
    def _():
        for r in range(bm):
            gather_copy(0, r, src_ref[0, r]).start()

    changed = jnp.logical_or(i == 0, be_ref[i] != be_ref[jnp.maximum(i - 1, 0)])

    @pl.when(jnp.logical_and(i < nreal, changed))
    def _():
        w1b[...] = w1_ref[...].astype(BF16)
        w2b[...] = w2_ref[...].astype(BF16)

    def first_half(sl):
        for r in range(bm):
            gather_copy(sl, r, 0).wait()
        for r in range(bm):
            gather_copy(1 - sl, r, srcn_ref[0, r]).start()
        hid = _dot(xbs[sl][...].astype(BF16), w1b[...]) + b1_ref[...]
        glu = jnp.minimum(hid[:, :D_FF], SWIGLU_LIMIT)
        lin = jnp.clip(hid[:, D_FF:], -SWIGLU_LIMIT, SWIGLU_LIMIT)
        act_scr[...] = (glu * jax.nn.sigmoid(SWIGLU_ALPHA * glu) * (lin + 1.0)).astype(BF16)

    def second_half(sl, issue, compute):
        if issue:
            for r in range(bm):
                scatter_copy(1 - sl, r, dstp_ref[0, r]).start()
        if compute:
            ybs[sl][...] = _dot(act_scr[...], w2b[...]) + b2_ref[...]

    for sl in range(2):
        @pl.when(jnp.logical_and(i < nreal, par == sl))
        def _(sl=sl):
            first_half(sl)

        @pl.when(jnp.logical_and(jnp.logical_and(i >= 2, i <= nreal + 1), par == sl))
        def _(sl=sl):
            for r in range(bm):
                scatter_copy(sl, r, 0).wait()

        @pl.when(jnp.logical_and(jnp.logical_and(i >= 1, i < nreal), par == sl))
        def _(sl=sl):
            second_half(sl, True, True)

        @pl.when(jnp.logical_and(i == nreal, par == sl))
        def _(sl=sl):
            second_half(sl, True, False)
            for r in range(bm):
                gather_copy(sl, r, 0).wait()

    @pl.when(jnp.logical_and(i == 0, nreal > 0))
    def _():
        second_half(0, False, True)


def _moe(h2, idx_t, lw, geo):
    n = h2.shape[0]
    d = D_MODEL
    bm = geo["BM"]
    a = n * TOP_K
    n_blocks = a // bm + N_EXPERTS + 2
    nb = jnp.int32(bm)

    e_flat = idx_t[:TOP_K].reshape(a)
    order = jnp.argsort(e_flat).astype(jnp.int32)
    counts = jnp.bincount(e_flat, length=N_EXPERTS).astype(jnp.int32)
    padded = (counts + bm - 1) // bm * bm
    padded_end = jnp.cumsum(padded)
    padded_start = padded_end - padded
    group_start = jnp.cumsum(counts) - counts
    blk = jnp.arange(n_blocks, dtype=jnp.int32)
    block_expert = jnp.minimum(jnp.sum((blk[:, None] * nb >= padded_end[None, :]).astype(jnp.int32), axis=1),
                               N_EXPERTS - 1)
    n_real = (padded_end[-1] // bm).astype(jnp.int32).reshape(1)
    lane = jnp.arange(bm, dtype=jnp.int32)[None, :]
    rank = (blk * nb - padded_start[block_expert])[:, None] + lane
    valid = rank < counts[block_expert][:, None]
    sp = jnp.clip(group_start[block_expert][:, None] + rank, 0, a - 1)
    assign = order[sp]
    slot_src = jnp.where(valid, assign % n, 0).astype(jnp.int32).reshape(n_blocks, 1, bm)
    slot_dst = jnp.where(valid, assign, a + lane).astype(jnp.int32).reshape(n_blocks, 1, bm)

    smem = lambda f: pl.BlockSpec((None, 1, bm), f, memory_space=pltpu.SMEM)
    return pl.pallas_call(
        functools.partial(_moe_kernel, bm=bm),
        grid_spec=pltpu.PrefetchScalarGridSpec(
            num_scalar_prefetch=2,
            grid=(n_blocks,),
            in_specs=[smem(lambda i, be, nr: (i, 0, 0)),
                      smem(lambda i, be, nr: (jnp.minimum(i + 1, n_blocks - 1), 0, 0)),
                      smem(lambda i, be, nr: (jnp.maximum(i - 1, 0), 0, 0)),
                      pl.BlockSpec(memory_space=pl.ANY),
                      pl.BlockSpec((None, d, 2 * D_FF), lambda i, be, nr: (be[i], 0, 0)),
                      pl.BlockSpec((None, 1, 2 * D_FF), lambda i, be, nr: (be[i], 0, 0)),
                      pl.BlockSpec((None, D_FF, d), lambda i, be, nr: (be[i], 0, 0)),
                      pl.BlockSpec((None, 1, d), lambda i, be, nr: (be[i], 0, 0))],
            out_specs=pl.BlockSpec(memory_space=pl.ANY),
            scratch_shapes=[pltpu.VMEM((bm, d), F32), pltpu.VMEM((bm, d), F32),
                            pltpu.VMEM((bm, d), F32), pltpu.VMEM((bm, d), F32),
                            pltpu.VMEM((d, 2 * D_FF), BF16), pltpu.VMEM((D_FF, d), BF16),
                            pltpu.VMEM((bm, D_FF), BF16),
                            pltpu.SemaphoreType.DMA((2,)), pltpu.SemaphoreType.DMA((2,))]),
        out_shape=jax.ShapeDtypeStruct((a + bm, d), F32),
        compiler_params=_cparams(("arbitrary",)),
        name="moe_experts",
    )(block_expert, n_real, slot_src, slot_src, slot_dst, h2, lw["moe_w1"], lw["moe_b1"], lw["moe_w2"], lw["moe_b2"])


def _final_kernel(x_ref, ya_ref, yb_ref, yc_ref, yd_ref, gt_ref, mod_ref, g_ref, o_ref):
    d = D_MODEL
    y = _gated_sum(gt_ref, (ya_ref, yb_ref, yc_ref, yd_ref))
    x = x_ref[...] + mod_ref[:, 5 * d:6 * d] * y
    o_ref[...] = _rms(x) * g_ref[...]


def _final(x, y4, gates, mod3, g_final, geo):
    tm, l, nl = geo["TM"], geo["L"], geo["NL"]
    d = D_MODEL
    row = lambda t: (t, 0)
    return pl.pallas_call(
        _final_kernel,
        grid=(nl // tm,),
        in_specs=[pl.BlockSpec((tm, d), row)] + _y4_specs(tm, nl) + [
                  pl.BlockSpec((tm, LANE), row),
                  pl.BlockSpec((None, 1, 6 * d), lambda t: ((t * tm) // l, 0, 0)),
                  pl.BlockSpec((1, d), lambda t: (0, 0))],
        out_specs=pl.BlockSpec((tm, d), row),
        out_shape=jax.ShapeDtypeStruct((nl, d), F32),
        compiler_params=_cparams(("arbitrary",)),
        name="final_norm",
    )(x, y4, y4, y4, y4, gates, mod3, g_final)


def _rot_cols(w, half):
    return jnp.concatenate([-w[:, half:2 * half], w[:, :half]], axis=1)


def _pack_layer(p, li):
    d = D_MODEL
    w_in = p["w_in"][li]
    o = 0
    a = w_in[:, 0:256]
    g = w_in[:, 256:512]
    cq = w_in[:, 512:896]
    rq = w_in[:, 896:1024]
    rg = w_in[:, 1024:1280]
    ckv = w_in[:, 1280:1536]
    kr = w_in[:, 1536:1600]
    rk = w_in[:, 1600:1728]
    rv = w_in[:, 1728:1984]
    del o

    def head_rot(w):
        return jnp.concatenate([_rot_cols(w[:, h * RET_DK:(h + 1) * RET_DK], RET_DK // 2)
                                for h in range(RET_HEADS)], axis=1)

    zpad = jnp.zeros((d, LANE - MLA_ROPE), F32)
    w_in_ext = jnp.concatenate(
        [a, g, cq, rq, head_rot(rq), rg, ckv, rk, head_rot(rk), rv,
         kr, zpad, _rot_cols(kr, MLA_ROPE // 2), zpad], axis=1).astype(BF16)

    w_uq = p["mla_w_uq"][li]
    hq = MLA_NOPE + MLA_ROPE
    zq = jnp.zeros((MLA_Q_RANK, LANE - MLA_ROPE), F32)
    wn = [w_uq[:, h * hq:h * hq + MLA_NOPE] for h in range(MLA_HEADS)]
    wr = [jnp.concatenate([w_uq[:, h * hq + MLA_NOPE:(h + 1) * hq], zq], axis=1) for h in range(MLA_HEADS)]
    wrt = [jnp.concatenate([_rot_cols(w_uq[:, h * hq + MLA_NOPE:(h + 1) * hq], MLA_ROPE // 2), zq], axis=1)
           for h in range(MLA_HEADS)]
    w_uq_ext = jnp.concatenate(wn + wr + wrt, axis=1).astype(BF16)

    w_ukv = p["mla_w_ukv"][li]
    hk = MLA_NOPE + MLA_V
    wk = [w_ukv[:, h * hk:h * hk + MLA_NOPE] for h in range(MLA_HEADS)]
    wv = [w_ukv[:, h * hk + MLA_NOPE:(h + 1) * hk] for h in range(MLA_HEADS)]
    w_ukv_ext = jnp.concatenate(wk + wv, axis=1).astype(BF16)

    lgf = jax.nn.log_sigmoid(p["ret_decay_fwd"][li].astype(F32))
    lgb = jax.nn.log_sigmoid(p["ret_decay_bwd"][li].astype(F32))
    lg = jnp.stack([lgf, lgb])
    lg_lane = jnp.repeat(lg, RET_DK, axis=1)

    w_out = p["w_out"][li].astype(BF16)
    rw = jnp.zeros((d, LANE), F32).at[:, :N_EXPERTS].set(p["router_w"][li])
    rw_hi = rw.astype(BF16)
    rb = jnp.full((1, LANE), NEG_BIG, F32).at[0, :N_EXPERTS].set(p["router_b"][li].astype(F32))
    return {
        "g1": p["g_norm1"][li].reshape(1, d), "g2": p["g_norm2"][li].reshape(1, d),
        "w_in": w_in_ext, "qn": p["mla_q_norm"][li].reshape(1, -1), "w_uq": w_uq_ext,
        "kvn": p["mla_kv_norm"][li].reshape(1, -1), "w_ukv": w_ukv_ext,
        "conv_w": p["conv_w"][li], "conv_b": p["conv_b"][li].reshape(1, -1),
        "conv_g": p["conv_ln_g"][li].reshape(1, -1), "conv_beta": p["conv_ln_b"][li].reshape(1, -1),
        "ret_lg": lg, "ret_lg_lane": lg_lane, "ret_lg_col": lg_lane.T,
        "w_out0": w_out[0:CONV_CH], "w_out1": w_out[CONV_CH:CONV_CH + MLA_HEADS * MLA_V],
        "w_out2": w_out[CONV_CH + MLA_HEADS * MLA_V:],
        "router_w_hi": rw_hi, "router_w_lo": (rw - rw_hi.astype(F32)).astype(BF16), "router_b": rb,
        "moe_w1": p["moe_w1"][li], "moe_b1": p["moe_b1"][li].reshape(N_EXPERTS, 1, -1),
        "moe_w2": p["moe_w2"][li], "moe_b2": p["moe_b2"][li].reshape(N_EXPERTS, 1, -1),
    }


def _position_tables(l, c, tm):
    rows = l // GRID_W
    row = jnp.repeat(jnp.arange(rows), GRID_W).astype(F32)
    col = jnp.tile(jnp.arange(GRID_W), rows).astype(F32)
    npa = MLA_ROPE // 4
    freq = ROPE_BASE ** (-jnp.arange(npa, dtype=F32) / npa)
    ang = jnp.concatenate([row[:, None] * freq, col[:, None] * freq], axis=-1)
    zl = jnp.zeros((l, LANE - MLA_ROPE), F32)
    mla_lat = jnp.concatenate([jnp.cos(ang), jnp.cos(ang), zl, jnp.sin(ang), jnp.sin(ang), zl], axis=1)
    mla_ctx = jnp.concatenate([jnp.ones((tm, LANE), F32), jnp.zeros((tm, LANE), F32)], axis=1)
    mla = jnp.concatenate([mla_lat, mla_ctx], axis=0)

    theta = 1.0 / (ROPE_BASE ** jnp.linspace(0.0, 1.0, RET_DK // 2, dtype=F32))

    def ret_tab(pos):
        a = pos.astype(F32)[:, None] * theta
        cs = jnp.tile(jnp.concatenate([jnp.cos(a), jnp.cos(a)], axis=1), (1, RET_HEADS))
        sn = jnp.tile(jnp.concatenate([jnp.sin(a), jnp.sin(a)], axis=1), (1, RET_HEADS))
        return jnp.concatenate([cs, sn], axis=1)

    ret = jnp.concatenate([ret_tab(c + jnp.arange(l)), jnp.tile(ret_tab(jnp.arange(c)), (tm // c, 1))], axis=0)
    return {"mla": mla, "ret": ret}


def kernel(x, c, ctx, c_ctx, w_ada, b_ada, g_norm1, g_norm2, w_in, w_out, conv_w, conv_b, conv_ln_g, conv_ln_b,
           mla_q_norm, mla_w_uq, mla_kv_norm, mla_w_ukv, ret_decay_fwd, ret_decay_bwd, router_w, router_b,
           moe_w1, moe_b1, moe_w2, moe_b2, g_final):
    b, l, d = x.shape
    cl = ctx.shape[1]
    depth = w_ada.shape[0]
    tm = 512
    nl, nc = b * l, b * cl
    nt = nl + nc
    assert d == D_MODEL and l % tm == 0 and nc % tm == 0 and tm % cl == 0 and cl % CONV_SUB == 0
    geo = {"B": b, "L": l, "C": cl, "NL": nl, "NC": nc, "NT": nt, "TM": tm, "NLT": nl // tm, "LT": l // tm,
           "TQ": min(1024, l), "TK": min(512, l), "BM": 512}
    p = {"w_in": w_in, "w_out": w_out, "g_norm1": g_norm1, "g_norm2": g_norm2, "conv_w": conv_w, "conv_b": conv_b,
         "conv_ln_g": conv_ln_g, "conv_ln_b": conv_ln_b, "mla_q_norm": mla_q_norm, "mla_w_uq": mla_w_uq,
         "mla_kv_norm": mla_kv_norm, "mla_w_ukv": mla_w_ukv, "ret_decay_fwd": ret_decay_fwd,
         "ret_decay_bwd": ret_decay_bwd, "router_w": router_w, "router_b": router_b, "moe_w1": moe_w1,
         "moe_b1": moe_b1, "moe_w2": moe_w2, "moe_b2": moe_b2}

    r = -(-(b + 1) // 8) * 8
    cs = jnp.zeros((r, d), F32).at[:b].set(c).at[b].set(c_ctx)
    mod = _modulation(cs, w_ada, b_ada)
    tabs = _position_tables(l, cl, tm)

    x_l, x_c = x.reshape(nl, d), ctx.reshape(nc, d)
    y4 = gates = None
    for li in range(depth):
        lw = _pack_layer(p, li)
        mod3 = mod[li].reshape(r, 1, 6 * d)
        update_ctx = li < depth - 1
        if y4 is None:
            z, q, k, v, rq, rk, rv, rg = _inproj((x_l, x_c), mod3, lw, tabs, geo)
        else:
            xa, z, q, k, v, rq, rk, rv, rg = _inproj(xa, mod3, lw, tabs, geo, y4=y4, gates=gates,
                                                     modp3=mod[li - 1].reshape(r, 1, 6 * d))
            x_l = x_c = xa
        n_rows = nt if update_ctx else nl
        yc = _conv(z, lw, geo, n_rows)
        om_l = _attention_latent(q, k, v, geo)
        if update_ctx:
            om_c = _attention_context(q, k, v, geo)
            yr_l, yr_c = _retention(rq, rk, rv, rg, lw, geo, True)
        else:
            (yr_l,) = _retention(rq, rk, rv, rg, lw, geo, False)
            om_c, yr_c = om_l, yr_l
        xa, h2, idx_t, gates = _outproj(yc, om_l, om_c, yr_l, yr_c, x_l, x_c, mod3, lw, geo, n_rows)
        y4 = _moe(h2, idx_t, lw, geo)
    out = _final(xa, y4, gates, mod[depth - 1].reshape(r, 1, 6 * d), g_final.reshape(1, d), geo)
    return out.reshape(b, l, d)
```

```python
import functools

import jax
import jax.numpy as jnp
from jax import lax
from jax.experimental import pallas as pl
from jax.experimental.pallas import tpu as pltpu

F32 = jnp.float32
BF16 = jnp.bfloat16

D_MODEL = 1024
CONV_CH = 256
CONV_WIDTH = 31
MLA_HEADS = 4
MLA_NOPE = 128
MLA_ROPE = 64
MLA_V = 128
MLA_Q_RANK = 384
MLA_KV_RANK = 256
RET_HEADS = 4
RET_DK = 32
RET_DV = 64
N_EXPERTS = 32
TOP_K = 4
D_FF = 1024
SWIGLU_LIMIT = 7.0
SWIGLU_ALPHA = 1.702
GRID_W = 64
ROPE_BASE = 10000.0
RMS_EPS = 1e-6
LN_EPS = 1e-5

LANE = 128
QK_DIM = 2 * LANE
V_EXT = 2 * LANE
ATT_ROWS = 64
LOG2_E = 1.4426950408889634
RET_QK = RET_HEADS * RET_DK
RET_V = RET_HEADS * RET_DV
CONV_HALO = 16
CONV_SUB = 256
RET_CHUNK = 128
NEG_BIG = -1e30
IDX_ROWS = 8
TOK_ROWS = D_MODEL // LANE

_O_A, _O_G, _O_CQ, _O_RQ, _O_RQT, _O_RG, _O_CKV, _O_RK, _O_RKT, _O_RV, _O_KR, _O_KRT, IN_EXT = (
    0, 256, 512, 896, 1024, 1152, 1408, 1664, 1792, 1920, 2176, 2304, 2432)

VMEM_LIMIT = 56 * 1024 * 1024


def _cparams(sem):
    return pltpu.CompilerParams(dimension_semantics=sem, vmem_limit_bytes=VMEM_LIMIT)


def _rms(x, eps=RMS_EPS):
    return x * lax.rsqrt(jnp.mean(x * x, axis=-1, keepdims=True) + eps)


def _dot(a, b, **kw):
    return jnp.dot(a, b, preferred_element_type=F32, **kw)


def _dot_nt(a, b):
    return lax.dot_general(a, b, (((1,), (1,)), ((), ())), preferred_element_type=F32)


def _mod_kernel(cs_ref, w_ref, b_ref, o_ref):
    s = cs_ref[...]
    s = s * jax.nn.sigmoid(s)
    o_ref[...] = _dot(s, w_ref[...], precision=lax.Precision.HIGHEST) + b_ref[...]


def _modulation(cs, w_ada, b_ada):
    depth, d, n = w_ada.shape
    r = cs.shape[0]
    tn = 1536
    return pl.pallas_call(
        _mod_kernel,
        grid=(depth, n // tn),
        in_specs=[pl.BlockSpec((r, d), lambda l, j: (0, 0)),
                  pl.BlockSpec((None, d, tn), lambda l, j: (l, 0, j)),
                  pl.BlockSpec((None, 1, tn), lambda l, j: (l, 0, j))],
        out_specs=pl.BlockSpec((None, r, tn), lambda l, j: (l, 0, j)),
        out_shape=jax.ShapeDtypeStruct((depth, r, n), F32),
        compiler_params=_cparams(("arbitrary", "arbitrary")),
        name="modulation",
    )(cs, w_ada, b_ada.reshape(depth, 1, n))


def _load_token_major(ref, n):
    return jnp.concatenate([ref[pl.ds(j, n, stride=TOK_ROWS), :] for j in range(TOK_ROWS)], axis=1)


def _store_token_major(ref, val):
    n = val.shape[0]
    for j in range(TOK_ROWS):
        ref[pl.ds(j, n, stride=TOK_ROWS), :] = val[:, j * LANE:(j + 1) * LANE]


def _gated_sum(gate_ref, y_refs):
    g = gate_ref[...]
    n = g.shape[0]
    terms = [g[:, kk:kk + 1] * _load_token_major(y_refs[kk], n) for kk in range(TOP_K)]
    return (terms[0] + terms[1]) + (terms[2] + terms[3])


def _inproj_kernel(*refs, combine, nlt):
    if combine:
        (x_ref, ya_ref, yb_ref, yc_ref, yd_ref, gt_ref, modp_ref, mod_ref, g1_ref, win_ref, qn_ref, wuq_ref,
         kvn_ref, wukv_ref, mcs_ref, rcs_ref, xo_ref, z_ref, q_ref, k_ref, v_ref, rq_ref, rk_ref, rv_ref,
         rg_ref) = refs
    else:
        (xl_ref, xc_ref, mod_ref, g1_ref, win_ref, qn_ref, wuq_ref, kvn_ref, wukv_ref,
         mcs_ref, rcs_ref, z_ref, q_ref, k_ref, v_ref, rq_ref, rk_ref, rv_ref, rg_ref) = refs
    d = D_MODEL
    if combine:
        y = _gated_sum(gt_ref, (ya_ref, yb_ref, yc_ref, yd_ref))
        x = x_ref[...] + modp_ref[:, 5 * d:6 * d] * y
        xo_ref[...] = x
    else:
        x = jnp.where(pl.program_id(0) < nlt, xl_ref[...], xc_ref[...])
    h = _rms(x) * g1_ref[...] * (1.0 + mod_ref[:, d:2 * d]) + mod_ref[:, 0:d]
    u = _dot(h.astype(BF16), win_ref[...])

    z_ref[...] = u[:, _O_A:_O_G] * jax.nn.sigmoid(u[:, _O_G:_O_CQ])

    mc = mcs_ref[:, 0:LANE]
    ms = mcs_ref[:, LANE:2 * LANE]
    rc = rcs_ref[:, 0:LANE]
    rs = rcs_ref[:, LANE:2 * LANE]

    cqn = _rms(u[:, _O_CQ:_O_RQ]) * qn_ref[...]
    uq = _dot(cqn.astype(BF16), wuq_ref[...])
    scale = float(MLA_NOPE + MLA_ROPE) ** -0.5 * LOG2_E
    hw = MLA_HEADS * LANE
    for hd in range(MLA_HEADS):
        lo = hd * LANE
        q_ref[hd, :, 0:LANE] = (uq[:, lo:lo + LANE] * scale).astype(BF16)
        rope = uq[:, hw + lo:hw + lo + LANE] * mc + uq[:, 2 * hw + lo:2 * hw + lo + LANE] * ms
        q_ref[hd, :, LANE:2 * LANE] = (rope * scale).astype(BF16)

    ckvn = _rms(u[:, _O_CKV:_O_RK]) * kvn_ref[...]
    ukv = _dot(ckvn.astype(BF16), wukv_ref[...])
    krope = (u[:, _O_KR:_O_KRT] * mc + u[:, _O_KRT:IN_EXT] * ms).astype(BF16)
    ones_col = (lax.broadcasted_iota(jnp.int32, krope.shape, 1) == 0).astype(BF16)
    for hd in range(MLA_HEADS):
        lo = hd * LANE
        k_ref[hd, :, 0:LANE] = ukv[:, lo:lo + LANE].astype(BF16)
        k_ref[hd, :, LANE:2 * LANE] = krope
        v_ref[hd, :, 0:LANE] = ukv[:, hw + lo:hw + lo + LANE].astype(BF16)
        v_ref[hd, :, LANE:2 * LANE] = ones_col

    rq_ref[...] = (u[:, _O_RQ:_O_RQT] * rc + u[:, _O_RQT:_O_RG] * rs).astype(BF16)
    rk_ref[...] = ((u[:, _O_RK:_O_RKT] * rc + u[:, _O_RKT:_O_RV] * rs) * (float(RET_DK) ** -0.5)).astype(BF16)
    rv_ref[...] = u[:, _O_RV:_O_KR].astype(BF16)
    rg_ref[...] = u[:, _O_RG:_O_CKV]


def _y4_specs(tm, n):
    return [pl.BlockSpec((tm * TOK_ROWS, LANE), functools.partial(lambda t, kk: (kk * (n // tm) + t, 0), kk=kk))
            for kk in range(TOP_K)]


def _inproj(x, mod3, lw, tabs, geo, y4=None, gates=None, modp3=None):
    nt, tm, nlt, lt = geo["NT"], geo["TM"], geo["NLT"], geo["LT"]
    b, l = geo["B"], geo["L"]
    d = D_MODEL
    combine = y4 is not None
    row = lambda t: (t, 0)
    const2 = lambda t: (0, 0)
    modrow = lambda t: (jnp.where(t < nlt, (t * tm) // l, b), 0, 0)
    tabrow = lambda t: (jnp.where(t < nlt, t % lt, lt), 0)
    if combine:
        in_specs = [pl.BlockSpec((tm, d), row)] + _y4_specs(tm, nt) + [
            pl.BlockSpec((tm, LANE), row), pl.BlockSpec((None, 1, 6 * d), modrow)]
        args = [x] + [y4] * TOP_K + [gates, modp3]
    else:
        in_specs = [pl.BlockSpec((tm, d), lambda t: (jnp.minimum(t, nlt - 1), 0)),
                    pl.BlockSpec((tm, d), lambda t: (jnp.maximum(t - nlt, 0), 0))]
        args = [x[0], x[1]]
    in_specs += [
        pl.BlockSpec((None, 1, 6 * d), modrow),
        pl.BlockSpec((1, d), const2),
        pl.BlockSpec((d, IN_EXT), const2),
        pl.BlockSpec((1, MLA_Q_RANK), const2),
        pl.BlockSpec((MLA_Q_RANK, 3 * MLA_HEADS * LANE), const2),
        pl.BlockSpec((1, MLA_KV_RANK), const2),
        pl.BlockSpec((MLA_KV_RANK, 2 * MLA_HEADS * LANE), const2),
        pl.BlockSpec((tm, 2 * LANE), tabrow),
        pl.BlockSpec((tm, 2 * LANE), tabrow),
    ]
    args += [mod3, lw["g1"], lw["w_in"], lw["qn"], lw["w_uq"], lw["kvn"], lw["w_ukv"],
             tabs["mla"], tabs["ret"]]
    out_specs, out_shape = [], []
    if combine:
        out_specs.append(pl.BlockSpec((tm, d), row))
        out_shape.append(jax.ShapeDtypeStruct((nt, d), F32))
    hrow = lambda t: (0, t, 0)
    out_specs += [
        pl.BlockSpec((tm, CONV_CH), row),
        pl.BlockSpec((MLA_HEADS, tm, QK_DIM), hrow),
        pl.BlockSpec((MLA_HEADS, tm, QK_DIM), hrow),
        pl.BlockSpec((MLA_HEADS, tm, V_EXT), hrow),
        pl.BlockSpec((tm, RET_QK), row),
        pl.BlockSpec((tm, RET_QK), row),
        pl.BlockSpec((tm, RET_V), row),
        pl.BlockSpec((tm, RET_V), row),
    ]
    out_shape += [
        jax.ShapeDtypeStruct((nt, CONV_CH), F32),
        jax.ShapeDtypeStruct((MLA_HEADS, nt, QK_DIM), BF16),
        jax.ShapeDtypeStruct((MLA_HEADS, nt, QK_DIM), BF16),
        jax.ShapeDtypeStruct((MLA_HEADS, nt, V_EXT), BF16),
        jax.ShapeDtypeStruct((nt, RET_QK), BF16),
        jax.ShapeDtypeStruct((nt, RET_QK), BF16),
        jax.ShapeDtypeStruct((nt, RET_V), BF16),
        jax.ShapeDtypeStruct((nt, RET_V), F32),
    ]
    return pl.pallas_call(
        functools.partial(_inproj_kernel, combine=combine, nlt=nlt),
        grid=(nt // tm,),
        in_specs=in_specs, out_specs=out_specs, out_shape=out_shape,
        compiler_params=_cparams(("arbitrary",)),
        name="inproj",
    )(*args)


def _conv_kernel(z_ref, zp_ref, zn_ref, w_ref, b_ref, g_ref, beta_ref, o_ref, e_scr, sh_scr, *, tm, l, c, nl):
    t = pl.program_id(0)
    nsub = tm // CONV_SUB
    rows = 64
    shr = CONV_SUB + 2 * CONV_HALO - 8
    for s in range(nsub):
        g0 = t * tm + s * CONV_SUB
        is_lat = g0 < nl
        pos0 = jnp.where(is_lat, lax.rem(g0, l), lax.rem(g0 - nl, c))
        seqlen = jnp.where(is_lat, l, c)
        first = pos0 == 0
        last = pos0 + CONV_SUB == seqlen
        lo = s * CONV_SUB
        prev = z_ref[lo - CONV_HALO:lo, :] if s > 0 else zp_ref[...]
        nxt = z_ref[lo + CONV_SUB:lo + CONV_SUB + CONV_HALO, :] if s < nsub - 1 else zn_ref[...]
        e_scr[0:CONV_HALO, :] = jnp.where(first, 0.0, prev)
        e_scr[CONV_HALO:CONV_HALO + CONV_SUB, :] = z_ref[lo:lo + CONV_SUB, :]
        e_scr[CONV_HALO + CONV_SUB:2 * CONV_HALO + CONV_SUB, :] = jnp.where(last, 0.0, nxt)
        for sft in range(1, 8):
            sh_scr[sft] = e_scr[sft:sft + shr, :]
        for r0 in range(0, CONV_SUB, rows):
            acc = jnp.zeros((rows, CONV_CH), F32) + b_ref[...]
            for j in range(CONV_WIDTH):
                off = r0 + j + CONV_HALO - CONV_WIDTH // 2
                sft = off % 8
                base = off - sft
                tap = e_scr[base:base + rows, :] if sft == 0 else sh_scr[sft, base:base + rows, :]
                acc = acc + w_ref[j:j + 1, :] * tap
            mu = jnp.mean(acc, axis=-1, keepdims=True)
            cen = acc - mu
            var = jnp.mean(cen * cen, axis=-1, keepdims=True)
            y = cen * lax.rsqrt(var + LN_EPS) * g_ref[...] + beta_ref[...]
            o_ref[lo + r0:lo + r0 + rows, :] = (y * jax.nn.sigmoid(y)).astype(o_ref.dtype)


def _conv(z, lw, geo, n_rows):
    tm, l, c, nl, nt = geo["TM"], geo["L"], geo["C"], geo["NL"], geo["NT"]
    hb = tm // CONV_HALO
    nhb = nt // CONV_HALO
    return pl.pallas_call(
        functools.partial(_conv_kernel, tm=tm, l=l, c=c, nl=nl),
        grid=(n_rows // tm,),
        in_specs=[pl.BlockSpec((tm, CONV_CH), lambda t: (t, 0)),
                  pl.BlockSpec((CONV_HALO, CONV_CH), lambda t: (jnp.maximum(t * hb - 1, 0), 0)),
                  pl.BlockSpec((CONV_HALO, CONV_CH), lambda t: (jnp.minimum((t + 1) * hb, nhb - 1), 0)),
                  pl.BlockSpec((CONV_WIDTH, CONV_CH), lambda t: (0, 0)),
                  pl.BlockSpec((1, CONV_CH), lambda t: (0, 0)),
                  pl.BlockSpec((1, CONV_CH), lambda t: (0, 0)),
                  pl.BlockSpec((1, CONV_CH), lambda t: (0, 0))],
        out_specs=pl.BlockSpec((tm, CONV_CH), lambda t: (t, 0)),
        out_shape=jax.ShapeDtypeStruct((n_rows, CONV_CH), BF16),
        scratch_shapes=[pltpu.VMEM((CONV_SUB + 2 * CONV_HALO, CONV_CH), F32),
                        pltpu.VMEM((8, CONV_SUB + 2 * CONV_HALO - 8, CONV_CH), F32)],
        compiler_params=_cparams(("arbitrary",)),
        name="conv_module",
    )(z, z, z, lw["conv_w"], lw["conv_b"], lw["conv_g"], lw["conv_beta"])


def _attn_kernel(*refs, with_latent, tk, n_lat_tiles):
    if with_latent:
        q_ref, kl_ref, vl_ref, kc_ref, vc_ref, o_ref, s0, s1, p0, p1, a0, a1, acc_scr, m_scr = refs
    else:
        q_ref, kc_ref, vc_ref, o_ref, s0, s1, p0, p1, a0, a1, acc_scr, m_scr = refs
    s_b, p_b, a_b = (s0, s1), (p0, p1), (a0, a1)
    tq = q_ref.shape[0]
    cw = kc_ref.shape[0]
    m_scr[...] = jnp.full(m_scr.shape, -jnp.inf, F32)
    acc_scr[...] = jnp.zeros(acc_scr.shape, F32)

    def scores(slot, k, width):
        s_b[slot][:, 0:width] = _dot_nt(q_ref[...], k)

    def softmax(slot, width):
        for r0 in range(0, tq, ATT_ROWS):
            rows = slice(r0, r0 + ATT_ROWS)
            s = s_b[slot][rows, 0:width]
            m_old = m_scr[rows, :]
            m_new = jnp.maximum(m_old, jnp.max(s, axis=-1, keepdims=True))
            a_b[slot][rows, :] = jnp.exp2(m_old - m_new)
            m_scr[rows, :] = m_new
            p_b[slot][rows, 0:width] = jnp.exp2(s - m_new).astype(BF16)

    def pv(slot, v, width):
        acc_scr[...] = acc_scr[...] * a_b[slot][...] + _dot(p_b[slot][:, 0:width], v)

    def k_tile(t):
        return kl_ref[pl.ds(pl.multiple_of(t * tk, tk), tk), :]

    def v_tile(t):
        return vl_ref[pl.ds(pl.multiple_of(t * tk, tk), tk), :]

    if with_latent:
        n = n_lat_tiles
        scores(0, kl_ref[0:tk, :], tk)
        scores(1, kl_ref[tk:2 * tk, :], tk)
        softmax(0, tk)

        def body(jj, carry):
            t = 2 * jj
            scores(0, k_tile(t + 2), tk)
            softmax(1, tk)
            pv(0, v_tile(t), tk)
            scores(1, k_tile(t + 3), tk)
            softmax(0, tk)
            pv(1, v_tile(t + 1), tk)
            return carry

        lax.fori_loop(0, n // 2 - 1, body, 0)
        scores(0, kc_ref[...], cw)
        softmax(1, tk)
        pv(0, vl_ref[(n - 2) * tk:(n - 1) * tk, :], tk)
        softmax(0, cw)
        pv(1, vl_ref[(n - 1) * tk:n * tk, :], tk)
        pv(0, vc_ref[...], cw)
    else:
        scores(0, kc_ref[...], cw)
        softmax(0, cw)
        pv(0, vc_ref[...], cw)
    acc = acc_scr[...]
    o_ref[...] = (acc[:, 0:MLA_V] / acc[:, MLA_V:MLA_V + 1]).astype(o_ref.dtype)


def _attn_scratch(tq, tk):
    return [pltpu.VMEM((tq, tk), F32), pltpu.VMEM((tq, tk), F32), pltpu.VMEM((tq, tk), BF16),
            pltpu.VMEM((tq, tk), BF16), pltpu.VMEM((tq, 1), F32), pltpu.VMEM((tq, 1), F32),
            pltpu.VMEM((tq, V_EXT), F32), pltpu.VMEM((tq, 1), F32)]


def _attention_latent(q, k, v, geo):
    b, l, c, nl, nt = geo["B"], geo["L"], geo["C"], geo["NL"], geo["NT"]
    tq, tk = geo["TQ"], geo["TK"]
    nq = l // tq
    cblk = nl // c
    assert (l // tk) % 2 == 0
    return pl.pallas_call(
        functools.partial(_attn_kernel, with_latent=True, tk=tk, n_lat_tiles=l // tk),
        grid=(b, MLA_HEADS, nq),
        in_specs=[pl.BlockSpec((None, tq, QK_DIM), lambda bi, h, i: (h, bi * nq + i, 0)),
                  pl.BlockSpec((None, l, QK_DIM), lambda bi, h, i: (h, bi, 0)),
                  pl.BlockSpec((None, l, V_EXT), lambda bi, h, i: (h, bi, 0)),
                  pl.BlockSpec((None, c, QK_DIM), lambda bi, h, i: (h, cblk + bi, 0)),
                  pl.BlockSpec((None, c, V_EXT), lambda bi, h, i: (h, cblk + bi, 0))],
        out_specs=pl.BlockSpec((tq, MLA_V), lambda bi, h, i: (bi * nq + i, h)),
        out_shape=jax.ShapeDtypeStruct((nl, MLA_HEADS * MLA_V), BF16),
        scratch_shapes=_attn_scratch(tq, tk),
        compiler_params=_cparams(("arbitrary", "arbitrary", "arbitrary")),
        name="mla_attention",
    )(q, k, v, k, v)


def _attention_context(q, k, v, geo):
    b, c, nl, nc = geo["B"], geo["C"], geo["NL"], geo["NC"]
    cblk = nl // c
    cmap = lambda bi, h: (h, cblk + bi, 0)
    return pl.pallas_call(
        functools.partial(_attn_kernel, with_latent=False, tk=c, n_lat_tiles=0),
        grid=(b, MLA_HEADS),
        in_specs=[pl.BlockSpec((None, c, QK_DIM), cmap),
                  pl.BlockSpec((None, c, QK_DIM), cmap),
                  pl.BlockSpec((None, c, V_EXT), cmap)],
        out_specs=pl.BlockSpec((c, MLA_V), lambda bi, h: (bi, h)),
        out_shape=jax.ShapeDtypeStruct((nc, MLA_HEADS * MLA_V), BF16),
        scratch_shapes=_attn_scratch(c, c),
        compiler_params=_cparams(("arbitrary", "arbitrary")),
        name="mla_attention_ctx",
    )(q, k, v)


def _ret_kernel(lg_ref, lgq_ref, lgcol_ref, rql_ref, rkl_ref, rvl_ref, rgl_ref,
                rqc_ref, rkc_ref, rvc_ref, rgc_ref, *rest, l, c, ctx_out):
    if ctx_out:
        yl_ref, yc_ref, sf_scr, m_scr = rest
    else:
        yl_ref, sf_scr, m_scr = rest
        yc_ref = None
    cn = RET_CHUNK
    ncc = c // cn
    ncl = l // cn
    fi = lax.broadcasted_iota(jnp.int32, (cn, 1), 0).astype(F32)
    lgf_q = lgq_ref[0:1, :]
    lgb_q = lgq_ref[1:2, :]
    qdec_f = jnp.exp((fi + 1.0) * lgf_q)
    qdec_b = jnp.exp((float(cn) - fi) * lgb_q)
    kdec_f = jnp.exp((float(cn - 1) - fi) * lgf_q)
    kdec_b = jnp.exp(fi * lgb_q)
    sdec_f = jnp.exp(float(cn) * lgcol_ref[:, 0:1])
    sdec_b = jnp.exp(float(cn) * lgcol_ref[:, 1:2])

    ri = lax.broadcasted_iota(jnp.int32, (cn, cn), 0)
    ci = lax.broadcasted_iota(jnp.int32, (cn, cn), 1)
    diff = (ri - ci).astype(F32)
    for hd in range(RET_HEADS):
        mf = jnp.where(diff >= 0, jnp.exp(jnp.maximum(diff, 0.0) * lg_ref[0, hd]), 0.0)
        mb = jnp.where(diff <= 0, jnp.exp(jnp.maximum(-diff, 0.0) * lg_ref[1, hd]), 0.0)
        m_scr[hd] = mf + mb

    lane_q = lax.broadcasted_iota(jnp.int32, (1, RET_QK), 1) // RET_DK
    lane_v = lax.broadcasted_iota(jnp.int32, (1, RET_V), 1) // RET_DV
    bd = (lax.broadcasted_iota(jnp.int32, (RET_QK, RET_V), 0) // RET_DK
          == lax.broadcasted_iota(jnp.int32, (RET_QK, RET_V), 1) // RET_DV)

    def kv_outer(kf, v, kdec):
        kt = (kf * kdec).T.astype(BF16)
        return jnp.where(bd, _dot(kt, v), 0.0)

    def out_chunk(q, k, v, g, s_f, s_b):
        qf = q.astype(F32)
        ps = []
        for hd in range(RET_HEADS):
            qm = jnp.where(lane_q == hd, q, jnp.zeros_like(q))
            ps.append((_dot_nt(qm, k) * m_scr[hd]).astype(BF16))
        pcat = jnp.concatenate(ps, axis=1)
        vbd = jnp.concatenate([jnp.where(lane_v == hd, v, jnp.zeros_like(v)) for hd in range(RET_HEADS)], axis=0)
        o = _dot(pcat, vbd)
        o = o + _dot((qf * qdec_f).astype(BF16), s_f.astype(BF16))
        o = o + _dot((qf * qdec_b).astype(BF16), s_b.astype(BF16))
        o2 = o * o
        ms = jnp.zeros_like(o)
        for hd in range(RET_HEADS):
            sel = lane_v == hd
            hs = jnp.sum(jnp.where(sel, o2, 0.0), axis=-1, keepdims=True) * (1.0 / RET_DV)
            ms = jnp.where(sel, hs, ms)
        return (g * jax.nn.sigmoid(g)) * (o * lax.rsqrt(ms + RMS_EPS))

    zero_s = jnp.zeros((RET_QK, RET_V), F32)

    s = zero_s
    for cc in range(ncc):
        sf_scr[cc] = s
        sl = slice(cc * cn, (cc + 1) * cn)
        s = s * sdec_f + kv_outer(rkc_ref[sl, :].astype(F32), rvc_ref[sl, :], kdec_f)

    def fwd_body(n, s):
        sf_scr[ncc + n] = s
        off = pl.multiple_of(n * cn, cn)
        return s * sdec_f + kv_outer(rkl_ref[pl.ds(off, cn), :].astype(F32), rvl_ref[pl.ds(off, cn), :], kdec_f)

    lax.fori_loop(0, ncl, fwd_body, s, unroll=8)

    s = zero_s
    for cc in reversed(range(ncc)):
        sl = slice(cc * cn, (cc + 1) * cn)
        k = rkc_ref[sl, :]
        v = rvc_ref[sl, :]
        if ctx_out:
            yc_ref[sl, :] = out_chunk(rqc_ref[sl, :], k, v, rgc_ref[sl, :], sf_scr[cc], s).astype(yc_ref.dtype)
        s = s * sdec_b + kv_outer(k.astype(F32), v, kdec_b)

    def bwd_body(i, s):
        n = ncl - 1 - i
        off = pl.multiple_of(n * cn, cn)
        k = rkl_ref[pl.ds(off, cn), :]
        v = rvl_ref[pl.ds(off, cn), :]
        y = out_chunk(rql_ref[pl.ds(off, cn), :], k, v, rgl_ref[pl.ds(off, cn), :], sf_scr[ncc + n], s)
        yl_ref[pl.ds(off, cn), :] = y.astype(yl_ref.dtype)
        return s * sdec_b + kv_outer(k.astype(F32), v, kdec_b)

    lax.fori_loop(0, ncl, bwd_body, s, unroll=8)


def _retention(rq, rk, rv, rg, lw, geo, ctx_out):
    b, l, c, nl = geo["B"], geo["L"], geo["C"], geo["NL"]
    cblk = nl // c
    lat = lambda w: pl.BlockSpec((l, w), lambda bi: (bi, 0))
    ctx = lambda w: pl.BlockSpec((c, w), lambda bi: (cblk + bi, 0))
    out_specs = [pl.BlockSpec((l, RET_V), lambda bi: (bi, 0))]
    out_shape = [jax.ShapeDtypeStruct((nl, RET_V), BF16)]
    if ctx_out:
        out_specs.append(pl.BlockSpec((c, RET_V), lambda bi: (bi, 0)))
        out_shape.append(jax.ShapeDtypeStruct((b * c, RET_V), BF16))
    nchunks = (l + c) // RET_CHUNK
    return pl.pallas_call(
        functools.partial(_ret_kernel, l=l, c=c, ctx_out=ctx_out),
        grid=(b,),
        in_specs=[pl.BlockSpec(memory_space=pltpu.SMEM),
                  pl.BlockSpec((2, RET_QK), lambda bi: (0, 0)),
                  pl.BlockSpec((RET_QK, 2), lambda bi: (0, 0)),
                  lat(RET_QK), lat(RET_QK), lat(RET_V), lat(RET_V),
                  ctx(RET_QK), ctx(RET_QK), ctx(RET_V), ctx(RET_V)],
        out_specs=out_specs, out_shape=out_shape,
        scratch_shapes=[pltpu.VMEM((nchunks, RET_QK, RET_V), F32),
                        pltpu.VMEM((RET_HEADS, RET_CHUNK, RET_CHUNK), F32)],
        compiler_params=_cparams(("arbitrary",)),
        name="retention",
    )(lw["ret_lg"], lw["ret_lg_lane"], lw["ret_lg_col"], rq, rk, rv, rg, rq, rk, rv, rg)


def _outproj_kernel(yc_ref, oml_ref, omc_ref, yrl_ref, yrc_ref, xl_ref, xc_ref, mod_ref, w0_ref, w1_ref, w2_ref,
                    g2_ref, rwh_ref, rwl_ref, rb_ref, xo_ref, h2_ref, idx_ref, gate_ref, *, nlt):
    d = D_MODEL
    is_lat = pl.program_id(0) < nlt
    om = jnp.where(is_lat, oml_ref[...], omc_ref[...])
    yr = jnp.where(is_lat, yrl_ref[...], yrc_ref[...])
    mix = _dot(yc_ref[...], w0_ref[...]) + _dot(om, w1_ref[...]) + _dot(yr, w2_ref[...])
    x = jnp.where(is_lat, xl_ref[...], xc_ref[...]) + mod_ref[:, 2 * d:3 * d] * mix
    xo_ref[...] = x
    h2 = _rms(x) * g2_ref[...] * (1.0 + mod_ref[:, 4 * d:5 * d]) + mod_ref[:, 3 * d:4 * d]
    _store_token_major(h2_ref, h2)
    h_hi = h2.astype(BF16)
    h_lo = (h2 - h_hi.astype(F32)).astype(BF16)
    lg = (_dot(h_hi, rwh_ref[...]) + (_dot(h_hi, rwl_ref[...]) + _dot(h_lo, rwh_ref[...]))) + rb_ref[...]
    lane = lax.broadcasted_iota(jnp.int32, lg.shape, 1)
    idx_out = jnp.zeros(lg.shape, jnp.int32)
    gate_out = jnp.zeros(lg.shape, F32)
    v0 = None
    den = None
    es = []
    for kk in range(TOP_K):
        m = jnp.max(lg, axis=-1, keepdims=True)
        idx = jnp.min(jnp.where(lg == m, lane, N_EXPERTS - 1), axis=-1, keepdims=True)
        lg = jnp.where(lane == idx, -jnp.inf, lg)
        if kk == 0:
            v0 = m
        e = jnp.exp(m - v0)
        es.append(e)
        den = e if den is None else den + e
        idx_out = jnp.where(lane == kk, idx, idx_out)
    for kk in range(TOP_K):
        gate_out = jnp.where(lane == kk, es[kk] / den, gate_out)
    idx_ref[...] = idx_out.T[0:IDX_ROWS, :]
    gate_ref[...] = gate_out


def _outproj(yc, om_l, om_c, yr_l, yr_c, x_l, x_c, mod3, lw, geo, n_rows):
    tm, nlt, l, b = geo["TM"], geo["NLT"], geo["L"], geo["B"]
    d = D_MODEL
    row = lambda t: (t, 0)
    latrow = lambda t: (jnp.minimum(t, nlt - 1), 0)
    ctxrow = lambda t: (jnp.maximum(t - nlt, 0), 0)
    const2 = lambda t: (0, 0)
    modrow = lambda t: (jnp.where(t < nlt, (t * tm) // l, b), 0, 0)
    return pl.pallas_call(
        functools.partial(_outproj_kernel, nlt=nlt),
        grid=(n_rows // tm,),
        in_specs=[pl.BlockSpec((tm, CONV_CH), row),
                  pl.BlockSpec((tm, MLA_HEADS * MLA_V), latrow),
                  pl.BlockSpec((tm, MLA_HEADS * MLA_V), ctxrow),
                  pl.BlockSpec((tm, RET_V), latrow),
                  pl.BlockSpec((tm, RET_V), ctxrow),
                  pl.BlockSpec((tm, d), latrow),
                  pl.BlockSpec((tm, d), ctxrow),
                  pl.BlockSpec((None, 1, 6 * d), modrow),
                  pl.BlockSpec((CONV_CH, d), const2),
                  pl.BlockSpec((MLA_HEADS * MLA_V, d), const2),
                  pl.BlockSpec((RET_V, d), const2),
                  pl.BlockSpec((1, d), const2),
                  pl.BlockSpec((d, LANE), const2),
                  pl.BlockSpec((d, LANE), const2),
                  pl.BlockSpec((1, LANE), const2)],
        out_specs=[pl.BlockSpec((tm, d), row), pl.BlockSpec((tm * TOK_ROWS, LANE), row),
                   pl.BlockSpec((IDX_ROWS, tm), lambda t: (0, t)), pl.BlockSpec((tm, LANE), row)],
        out_shape=[jax.ShapeDtypeStruct((n_rows, d), F32), jax.ShapeDtypeStruct((n_rows * TOK_ROWS, LANE), F32),
                   jax.ShapeDtypeStruct((IDX_ROWS, n_rows), jnp.int32), jax.ShapeDtypeStruct((n_rows, LANE), F32)],
        compiler_params=_cparams(("arbitrary",)),
        name="outproj_router",
    )(yc, om_l, om_c, yr_l, yr_c, x_l, x_c, mod3, lw["w_out0"], lw["w_out1"], lw["w_out2"], lw["g2"], lw["router_w_hi"], lw["router_w_lo"], lw["router_b"])


def _moe_kernel(be_ref, nreal_ref, src_ref, srcn_ref, dstp_ref, h2_hbm, w1_ref, b1_ref, w2_ref, b2_ref,
                y4_hbm, xb0, xb1, yb0, yb1, w1b, w2b, act_scr, sem_g, sem_s, *, bm):
    i = pl.program_id(0)
    nreal = nreal_ref[0]
    par = lax.rem(i, 2)
    xbs, ybs = (xb0, xb1), (yb0, yb1)

    def tok_rows(t):
        return pl.ds(pl.multiple_of(t * TOK_ROWS, TOK_ROWS), TOK_ROWS)

    def gather_copy(sl, r, tok):
        return pltpu.make_async_copy(h2_hbm.at[tok_rows(tok), :], xbs[sl].at[pl.ds(r * TOK_ROWS, TOK_ROWS), :],
                                     sem_g.at[sl])

    def scatter_copy(sl, r, row):
        return pltpu.make_async_copy(ybs[sl].at[pl.ds(r * TOK_ROWS, TOK_ROWS), :], y4_hbm.at[tok_rows(row), :],
                                     sem_s.at[sl])

    @pl.when(i == 0)
    def _():
        for r in range(bm):
            gather_copy(0, r, src_ref[0, r]).start()

    changed = jnp.logical_or(i == 0, be_ref[i] != be_ref[jnp.maximum(i - 1, 0)])

    @pl.when(jnp.logical_and(i < nreal, changed))
    def _():
        w1b[...] = w1_ref[...].astype(BF16)
        w2b[...] = w2_ref[...].astype(BF16)

    def first_half(sl):
        for r in range(bm):
            gather_copy(sl, r, 0).wait()
        for r in range(bm):
            gather_copy(1 - sl, r, srcn_ref[0, r]).start()
        hid = _dot(_load_token_major(xbs[sl], bm).astype(BF16), w1b[...]) + b1_ref[...]
        glu = jnp.minimum(hid[:, :D_FF], SWIGLU_LIMIT)
        lin = jnp.clip(hid[:, D_FF:], -SWIGLU_LIMIT, SWIGLU_LIMIT)
        act_scr[...] = (glu * jax.nn.sigmoid(SWIGLU_ALPHA * glu) * (lin + 1.0)).astype(BF16)

    def second_half(sl, issue, compute):
        if issue:
            for r in range(bm):
                scatter_copy(1 - sl, r, dstp_ref[0, r]).start()
        if compute:
            _store_token_major(ybs[sl], _dot(act_scr[...], w2b[...]) + b2_ref[...])

    for sl in range(2):
        @pl.when(jnp.logical_and(i < nreal, par == sl))
        def _(sl=sl):
            first_half(sl)

        @pl.when(jnp.logical_and(jnp.logical_and(i >= 2, i <= nreal + 1), par == sl))
        def _(sl=sl):
            for r in range(bm):
                scatter_copy(sl, r, 0).wait()

        @pl.when(jnp.logical_and(jnp.logical_and(i >= 1, i < nreal), par == sl))
        def _(sl=sl):
            second_half(sl, True, True)

        @pl.when(jnp.logical_and(i == nreal, par == sl))
        def _(sl=sl):
            second_half(sl, True, False)
            for r in range(bm):
                gather_copy(sl, r, 0).wait()

    @pl.when(jnp.logical_and(i == 0, nreal > 0))
    def _():
        second_half(0, False, True)


def _moe(h2, idx_t, lw, li, geo):
    n = h2.shape[0] // TOK_ROWS
    d = D_MODEL
    bm = geo["BM"]
    a = n * TOP_K
    n_blocks = a // bm + N_EXPERTS + 2
    nb = jnp.int32(bm)

    e_flat = idx_t[:TOP_K].reshape(a)
    order = jnp.argsort(e_flat).astype(jnp.int32)
    counts = jnp.bincount(e_flat, length=N_EXPERTS).astype(jnp.int32)
    padded = (counts + bm - 1) // bm * bm
    padded_end = jnp.cumsum(padded)
    padded_start = padded_end - padded
    group_start = jnp.cumsum(counts) - counts
    blk = jnp.arange(n_blocks, dtype=jnp.int32)
    block_expert = jnp.minimum(jnp.sum((blk[:, None] * nb >= padded_end[None, :]).astype(jnp.int32), axis=1),
                               N_EXPERTS - 1)
    n_real = (padded_end[-1] // bm).astype(jnp.int32).reshape(1)
    lane = jnp.arange(bm, dtype=jnp.int32)[None, :]
    rank = (blk * nb - padded_start[block_expert])[:, None] + lane
    valid = rank < counts[block_expert][:, None]
    sp = jnp.clip(group_start[block_expert][:, None] + rank, 0, a - 1)
    assign = order[sp]
    slot_src = jnp.where(valid, assign % n, 0).astype(jnp.int32).reshape(n_blocks, 1, bm)
    slot_dst = jnp.where(valid, assign, a + lane).astype(jnp.int32).reshape(n_blocks, 1, bm)

    smem = lambda f: pl.BlockSpec((None, 1, bm), f, memory_space=pltpu.SMEM)
    return pl.pallas_call(
        functools.partial(_moe_kernel, bm=bm),
        grid_spec=pltpu.PrefetchScalarGridSpec(
            num_scalar_prefetch=2,
            grid=(n_blocks,),
            in_specs=[smem(lambda i, be, nr: (i, 0, 0)),
                      smem(lambda i, be, nr: (jnp.minimum(i + 1, n_blocks - 1), 0, 0)),
                      smem(lambda i, be, nr: (jnp.maximum(i - 1, 0), 0, 0)),
                      pl.BlockSpec(memory_space=pl.ANY),
                      pl.BlockSpec((None, None, d, 2 * D_FF), lambda i, be, nr: (li, be[i], 0, 0)),
                      pl.BlockSpec((None, None, 1, 2 * D_FF), lambda i, be, nr: (li, be[i], 0, 0)),
                      pl.BlockSpec((None, None, D_FF, d), lambda i, be, nr: (li, be[i], 0, 0)),
                      pl.BlockSpec((None, None, 1, d), lambda i, be, nr: (li, be[i], 0, 0))],
            out_specs=pl.BlockSpec(memory_space=pl.ANY),
            scratch_shapes=[pltpu.VMEM((bm * TOK_ROWS, LANE), F32), pltpu.VMEM((bm * TOK_ROWS, LANE), F32),
                            pltpu.VMEM((bm * TOK_ROWS, LANE), F32), pltpu.VMEM((bm * TOK_ROWS, LANE), F32),
                            pltpu.VMEM((d, 2 * D_FF), BF16), pltpu.VMEM((D_FF, d), BF16),
                            pltpu.VMEM((bm, D_FF), BF16),
                            pltpu.SemaphoreType.DMA((2,)), pltpu.SemaphoreType.DMA((2,))]),
        out_shape=jax.ShapeDtypeStruct(((a + bm) * TOK_ROWS, LANE), F32),
        compiler_params=_cparams(("arbitrary",)),
        name="moe_experts",
    )(block_expert, n_real, slot_src, slot_src, slot_dst, h2, lw["moe_w1"], lw["moe_b1"], lw["moe_w2"], lw["moe_b2"])


def _final_kernel(x_ref, ya_ref, yb_ref, yc_ref, yd_ref, gt_ref, mod_ref, g_ref, o_ref):
    d = D_MODEL
    y = _gated_sum(gt_ref, (ya_ref, yb_ref, yc_ref, yd_ref))
    x = x_ref[...] + mod_ref[:, 5 * d:6 * d] * y
    o_ref[...] = _rms(x) * g_ref[...]


def _final(x, y4, gates, mod3, g_final, geo):
    tm, l, nl = geo["TM"], geo["L"], geo["NL"]
    d = D_MODEL
    row = lambda t: (t, 0)
    return pl.pallas_call(
        _final_kernel,
        grid=(nl // tm,),
        in_specs=[pl.BlockSpec((tm, d), row)] + _y4_specs(tm, nl) + [
                  pl.BlockSpec((tm, LANE), row),
                  pl.BlockSpec((None, 1, 6 * d), lambda t: ((t * tm) // l, 0, 0)),
                  pl.BlockSpec((1, d), lambda t: (0, 0))],
        out_specs=pl.BlockSpec((tm, d), row),
        out_shape=jax.ShapeDtypeStruct((nl, d), F32),
        compiler_params=_cparams(("arbitrary",)),
        name="final_norm",
    )(x, y4, y4, y4, y4, gates, mod3, g_final)


def _rot_cols(w, half):
    return jnp.concatenate([-w[:, half:2 * half], w[:, :half]], axis=1)


def _pack_layer(p, li):
    d = D_MODEL
    w_in = p["w_in"][li]
    o = 0
    a = w_in[:, 0:256]
    g = w_in[:, 256:512]
    cq = w_in[:, 512:896]
    rq = w_in[:, 896:1024]
    rg = w_in[:, 1024:1280]
    ckv = w_in[:, 1280:1536]
    kr = w_in[:, 1536:1600]
    rk = w_in[:, 1600:1728]
    rv = w_in[:, 1728:1984]
    del o

    def head_rot(w):
        return jnp.concatenate([_rot_cols(w[:, h * RET_DK:(h + 1) * RET_DK], RET_DK // 2)
                                for h in range(RET_HEADS)], axis=1)

    zpad = jnp.zeros((d, LANE - MLA_ROPE), F32)
    w_in_ext = jnp.concatenate(
        [a, g, cq, rq, head_rot(rq), rg, ckv, rk, head_rot(rk), rv,
         kr, zpad, _rot_cols(kr, MLA_ROPE // 2), zpad], axis=1).astype(BF16)

    w_uq = p["mla_w_uq"][li]
    hq = MLA_NOPE + MLA_ROPE
    zq = jnp.zeros((MLA_Q_RANK, LANE - MLA_ROPE), F32)
    wn = [w_uq[:, h * hq:h * hq + MLA_NOPE] for h in range(MLA_HEADS)]
    wr = [jnp.concatenate([w_uq[:, h * hq + MLA_NOPE:(h + 1) * hq], zq], axis=1) for h in range(MLA_HEADS)]
    wrt = [jnp.concatenate([_rot_cols(w_uq[:, h * hq + MLA_NOPE:(h + 1) * hq], MLA_ROPE // 2), zq], axis=1)
           for h in range(MLA_HEADS)]
    w_uq_ext = jnp.concatenate(wn + wr + wrt, axis=1).astype(BF16)

    w_ukv = p["mla_w_ukv"][li]
    hk = MLA_NOPE + MLA_V
    wk = [w_ukv[:, h * hk:h * hk + MLA_NOPE] for h in range(MLA_HEADS)]
    wv = [w_ukv[:, h * hk + MLA_NOPE:(h + 1) * hk] for h in range(MLA_HEADS)]
    w_ukv_ext = jnp.concatenate(wk + wv, axis=1).astype(BF16)

    lgf = jax.nn.log_sigmoid(p["ret_decay_fwd"][li].astype(F32))
    lgb = jax.nn.log_sigmoid(p["ret_decay_bwd"][li].astype(F32))
    lg = jnp.stack([lgf, lgb])
    lg_lane = jnp.repeat(lg, RET_DK, axis=1)

    w_out = p["w_out"][li].astype(BF16)
    rw = jnp.zeros((d, LANE), F32).at[:, :N_EXPERTS].set(p["router_w"][li])
    rw_hi = rw.astype(BF16)
    rb = jnp.full((1, LANE), NEG_BIG, F32).at[0, :N_EXPERTS].set(p["router_b"][li].astype(F32))
    return {
        "g1": p["g_norm1"][li].reshape(1, d), "g2": p["g_norm2"][li].reshape(1, d),
        "w_in": w_in_ext, "qn": p["mla_q_norm"][li].reshape(1, -1), "w_uq": w_uq_ext,
        "kvn": p["mla_kv_norm"][li].reshape(1, -1), "w_ukv": w_ukv_ext,
        "conv_w": p["conv_w"][li], "conv_b": p["conv_b"][li].reshape(1, -1),
        "conv_g": p["conv_ln_g"][li].reshape(1, -1), "conv_beta": p["conv_ln_b"][li].reshape(1, -1),
        "ret_lg": lg, "ret_lg_lane": lg_lane, "ret_lg_col": lg_lane.T,
        "w_out0": w_out[0:CONV_CH], "w_out1": w_out[CONV_CH:CONV_CH + MLA_HEADS * MLA_V],
        "w_out2": w_out[CONV_CH + MLA_HEADS * MLA_V:],
        "router_w_hi": rw_hi, "router_w_lo": (rw - rw_hi.astype(F32)).astype(BF16), "router_b": rb,
        "moe_w1": p["moe_w1"], "moe_b1": p["moe_b1"][:, :, None, :],
        "moe_w2": p["moe_w2"], "moe_b2": p["moe_b2"][:, :, None, :],
    }


def _position_tables(l, c, tm):
    rows = l // GRID_W
    row = jnp.repeat(jnp.arange(rows), GRID_W).astype(F32)
    col = jnp.tile(jnp.arange(GRID_W), rows).astype(F32)
    npa = MLA_ROPE // 4
    freq = ROPE_BASE ** (-jnp.arange(npa, dtype=F32) / npa)
    ang = jnp.concatenate([row[:, None] * freq, col[:, None] * freq], axis=-1)
    zl = jnp.zeros((l, LANE - MLA_ROPE), F32)
    mla_lat = jnp.concatenate([jnp.cos(ang), jnp.cos(ang), zl, jnp.sin(ang), jnp.sin(ang), zl], axis=1)
    mla_ctx = jnp.concatenate([jnp.ones((tm, LANE), F32), jnp.zeros((tm, LANE), F32)], axis=1)
    mla = jnp.concatenate([mla_lat, mla_ctx], axis=0)

    theta = 1.0 / (ROPE_BASE ** jnp.linspace(0.0, 1.0, RET_DK // 2, dtype=F32))

    def ret_tab(pos):
        a = pos.astype(F32)[:, None] * theta
        cs = jnp.tile(jnp.concatenate([jnp.cos(a), jnp.cos(a)], axis=1), (1, RET_HEADS))
        sn = jnp.tile(jnp.concatenate([jnp.sin(a), jnp.sin(a)], axis=1), (1, RET_HEADS))
        return jnp.concatenate([cs, sn], axis=1)

    ret = jnp.concatenate([ret_tab(c + jnp.arange(l)), jnp.tile(ret_tab(jnp.arange(c)), (tm // c, 1))], axis=0)
    return {"mla": mla, "ret": ret}


def kernel(x, c, ctx, c_ctx, w_ada, b_ada, g_norm1, g_norm2, w_in, w_out, conv_w, conv_b, conv_ln_g, conv_ln_b,
           mla_q_norm, mla_w_uq, mla_kv_norm, mla_w_ukv, ret_decay_fwd, ret_decay_bwd, router_w, router_b,
           moe_w1, moe_b1, moe_w2, moe_b2, g_final):
    b, l, d = x.shape
    cl = ctx.shape[1]
    depth = w_ada.shape[0]
    tm = 512
    nl, nc = b * l, b * cl
    nt = nl + nc
    assert d == D_MODEL and l % tm == 0 and nc % tm == 0 and tm % cl == 0 and cl % CONV_SUB == 0
    geo = {"B": b, "L": l, "C": cl, "NL": nl, "NC": nc, "NT": nt, "TM": tm, "NLT": nl // tm, "LT": l // tm,
           "TQ": min(1024, l), "TK": min(512, l), "BM": 512}
    p = {"w_in": w_in, "w_out": w_out, "g_norm1": g_norm1, "g_norm2": g_norm2, "conv_w": conv_w, "conv_b": conv_b,
         "conv_ln_g": conv_ln_g, "conv_ln_b": conv_ln_b, "mla_q_norm": mla_q_norm, "mla_w_uq": mla_w_uq,
         "mla_kv_norm": mla_kv_norm, "mla_w_ukv": mla_w_ukv, "ret_decay_fwd": ret_decay_fwd,
         "ret_decay_bwd": ret_decay_bwd, "router_w": router_w, "router_b": router_b, "moe_w1": moe_w1,
         "moe_b1": moe_b1, "moe_w2": moe_w2, "moe_b2": moe_b2}

    r = -(-(b + 1) // 8) * 8
    cs = jnp.zeros((r, d), F32).at[:b].set(c).at[b].set(c_ctx)
    mod = _modulation(cs, w_ada, b_ada)
    tabs = _position_tables(l, cl, tm)

    x_l, x_c = x.reshape(nl, d), ctx.reshape(nc, d)
    y4 = gates = None
    for li in range(depth):
        lw = _pack_layer(p, li)
        mod3 = mod[li].reshape(r, 1, 6 * d)
        update_ctx = li < depth - 1
        if y4 is None:
            z, q, k, v, rq, rk, rv, rg = _inproj((x_l, x_c), mod3, lw, tabs, geo)
        else:
            xa, z, q, k, v, rq, rk, rv, rg = _inproj(xa, mod3, lw, tabs, geo, y4=y4, gates=gates,
                                                     modp3=mod[li - 1].reshape(r, 1, 6 * d))
            x_l = x_c = xa
        n_rows = nt if update_ctx else nl
        yc = _conv(z, lw, geo, n_rows)
        om_l = _attention_latent(q, k, v, geo)
        if update_ctx:
            om_c = _attention_context(q, k, v, geo)
            yr_l, yr_c = _retention(rq, rk, rv, rg, lw, geo, True)
        else:
            (yr_l,) = _retention(rq, rk, rv, rg, lw, geo, False)
            om_c, yr_c = om_l, yr_l
        xa, h2, idx_t, gates = _outproj(yc, om_l, om_c, yr_l, yr_c, x_l, x_c, mod3, lw, geo, n_rows)
        y4 = _moe(h2, idx_t, lw, li, geo)
    out = _final(xa, y4, gates, mod[depth - 1].reshape(r, 1, 6 * d), g_final.reshape(1, d), geo)
    return out.reshape(b, l, d)
```
